```python
import math
import jax, jax.numpy as jnp
from jax import lax
import numpy as np

D_MODEL = 1024
BATCH = 32
SEQ = 2048
DEPTH = 2

CHUNK = 64

N_A = DEPTH // 2
N_B = DEPTH - N_A

SSM_EXPAND = 2
D_INNER = SSM_EXPAND * D_MODEL
SSM_HEAD_DIM = 64
SSM_HEADS = D_INNER // SSM_HEAD_DIM
SSM_GROUPS = 8
HEADS_PER_GROUP = SSM_HEADS // SSM_GROUPS
SSM_STATE = 128
CONV_K = 4
CONV_DIM = D_INNER + 2 * SSM_GROUPS * SSM_STATE
D_IN_PROJ = D_INNER + CONV_DIM + SSM_HEADS
DT_MIN = 0.001
DT_MAX = 0.1

ATT_HEADS = 16
ATT_HEAD_DIM = 64
ATT_DIM = ATT_HEADS * ATT_HEAD_DIM
Q_BLOCK = 128
FORGET_BIAS_MEAN = 3.0

D_FF = ((8 * D_MODEL + 3 * 256 - 1) // (3 * 256)) * 256

DEEPNORM_ALPHA = (2.0 * DEPTH) ** 0.25
DEEPNORM_BETA = (8.0 * DEPTH) ** -0.25

LN_EPS = 1e-5
RMS_EPS = 1e-5

kernel_name = "yoco_ssd_fox_deepnorm_trunk"


def _layer_norm(x, g, b):
    xf = x.astype(jnp.float32)
    mu = jnp.mean(xf, axis=-1, keepdims=True)
    var = jnp.mean(jnp.square(xf - mu), axis=-1, keepdims=True)
    y = (xf - mu) * lax.rsqrt(var + LN_EPS) * g.astype(jnp.float32) + b.astype(jnp.float32)
    return y.astype(x.dtype)


def _swiglu(x, w_gate, w_up, w_down):
    return (jax.nn.silu(x @ w_gate) * (x @ w_up)) @ w_down


def _causal_depthwise_conv(u, w, b):
    k = w.shape[0]
    out = lax.conv_general_dilated(
        u, w[:, None, :].astype(u.dtype), window_strides=(1,), padding=[(k - 1, 0)],
        dimension_numbers=('NWC', 'WIO', 'NWC'), feature_group_count=u.shape[-1])
    return out + b


def _ssd_scan(xh, dt, A, Bm, Cm):
    bsz, seq = xh.shape[0], xh.shape[1]
    nc = seq // CHUNK

    def to_chunks(t):
        return jnp.moveaxis(t.reshape((bsz, nc, CHUNK) + t.shape[2:]), 1, 0)

    xc, dtc, Bc, Cc = to_chunks(xh), to_chunks(dt), to_chunks(Bm), to_chunks(Cm)
    causal = jnp.tril(jnp.ones((CHUNK, CHUNK), dtype=bool))[None, :, :, None, None]

    def step(state, inp):
        x_c, dt_c, B_c, C_c = inp
        cum = jnp.cumsum(dt_c * A, axis=1)
        seg = cum[:, :, None] - cum[:, None, :]
        decay = jnp.exp(jnp.where(causal, seg, -jnp.inf))
        cb = jnp.einsum('blgn,bsgn->blsg', C_c, B_c)
        y_intra = jnp.einsum('blsg,blsgr,bsgr,bsgrp->blgrp', cb, decay, dt_c, x_c)
        y_state = jnp.einsum('blgn,bgrpn->blgrp', C_c, state) * jnp.exp(cum)[..., None]
        w_end = jnp.exp(cum[:, -1:] - cum) * dt_c
        new_state = state * jnp.exp(cum[:, -1])[..., None, None] + jnp.einsum(
            'bsgn,bsgr,bsgrp->bgrpn', B_c, w_end, x_c)
        return new_state, y_intra + y_state

    state0 = jnp.zeros((bsz, SSM_GROUPS, HEADS_PER_GROUP, SSM_HEAD_DIM, SSM_STATE), jnp.float32)
    _, y = lax.scan(step, state0, (xc, dtc, Bc, Cc))
    return jnp.moveaxis(y, 0, 1).reshape(xh.shape)


def _mamba2_mixer(x, in_w, conv_w, conv_b, dt_bias, a_log, d_skip, norm_w, out_w):
    f32 = jnp.float32
    bsz, seq, _ = x.shape
    proj = x @ in_w
    z, xbc, dt_raw = jnp.split(proj, [D_INNER, D_INNER + CONV_DIM], axis=-1)
    xbc = jax.nn.silu(_causal_depthwise_conv(xbc, conv_w, conv_b))
    xs, Bm, Cm = jnp.split(xbc, [D_INNER, D_INNER + SSM_GROUPS * SSM_STATE], axis=-1)
    xh = xs.reshape(bsz, seq, SSM_GROUPS, HEADS_PER_GROUP, SSM_HEAD_DIM).astype(f32)
    Bm = Bm.reshape(bsz, seq, SSM_GROUPS, SSM_STATE).astype(f32)
    Cm = Cm.reshape(bsz, seq, SSM_GROUPS, SSM_STATE).astype(f32)
    dt = jax.nn.softplus(dt_raw.astype(f32) + dt_bias.astype(f32))
    dt = dt.reshape(bsz, seq, SSM_GROUPS, HEADS_PER_GROUP)
    A = -jnp.exp(a_log.astype(f32)).reshape(SSM_GROUPS, HEADS_PER_GROUP)
    y = _ssd_scan(xh, dt, A, Bm, Cm)
    y = y + d_skip.astype(f32).reshape(SSM_GROUPS, HEADS_PER_GROUP)[..., None] * xh
    y = y.reshape(bsz, seq, D_INNER) * jax.nn.silu(z.astype(f32))
    yg = y.reshape(bsz, seq, SSM_GROUPS, D_INNER // SSM_GROUPS)
    yg = yg * lax.rsqrt(jnp.mean(jnp.square(yg), axis=-1, keepdims=True) + RMS_EPS)
    y = yg.reshape(bsz, seq, D_INNER) * norm_w.astype(f32)
    return y.astype(x.dtype) @ out_w


def _shared_kv(x, kv_w, kv_b_f):
    bsz, seq, _ = x.shape
    kvf = x @ kv_w
    k, v, f = jnp.split(kvf, [ATT_DIM, 2 * ATT_DIM], axis=-1)
    k = k.reshape(bsz, seq, ATT_HEADS, ATT_HEAD_DIM)
    v = v.reshape(bsz, seq, ATT_HEADS, ATT_HEAD_DIM)
    log_f = jax.nn.log_sigmoid(f.astype(jnp.float32) + kv_b_f.astype(jnp.float32))
    fcum = jnp.transpose(jnp.cumsum(log_f, axis=1), (0, 2, 1))
    return k, v, fcum


def _forgetting_attention(x, q_w, o_w, k, v, fcum):
    bsz, seq, _ = x.shape
    q = (x @ q_w).reshape(bsz, seq, ATT_HEADS, ATT_HEAD_DIM) * (ATT_HEAD_DIM ** -0.5)
    outs = []
    for blk in range(seq // Q_BLOCK):
        lo, hi = blk * Q_BLOCK, (blk + 1) * Q_BLOCK
        s = jnp.einsum('bqhd,bkhd->bhqk', q[:, lo:hi], k[:, :hi]).astype(jnp.float32)
        bias = fcum[:, :, lo:hi, None] - fcum[:, :, None, :hi]
        mask = jnp.arange(hi)[None, :] <= jnp.arange(lo, hi)[:, None]
        p = jax.nn.softmax(jnp.where(mask, s + bias, -jnp.inf), axis=-1).astype(v.dtype)
        outs.append(jnp.einsum('bhqk,bkhd->bqhd', p, v[:, :hi]))
    o = jnp.concatenate(outs, axis=1).reshape(bsz, seq, ATT_DIM)
    return o @ o_w


def setup_inputs(seed: int = 0) -> dict:
    key = jax.random.key(seed)
    ks = jax.random.split(key, 24)
    f32 = jnp.float32
    nrm = lambda k, shape, scale: jax.random.normal(k, shape, f32) * scale
    beta = DEEPNORM_BETA

    x = jax.random.normal(ks[0], (BATCH, SEQ, D_MODEL), f32)

    ssm_in_w = nrm(ks[1], (N_A, D_MODEL, D_IN_PROJ), D_MODEL ** -0.5)
    ssm_conv_w = nrm(ks[2], (N_A, CONV_K, CONV_DIM), CONV_K ** -0.5)
    ssm_conv_b = nrm(ks[3], (N_A, CONV_DIM), 0.02)
    dt0 = jnp.exp(jax.random.uniform(ks[4], (N_A, SSM_HEADS), f32,
                                     math.log(DT_MIN), math.log(DT_MAX)))
    ssm_dt_bias = dt0 + jnp.log(-jnp.expm1(-dt0))
    ssm_a_log = jnp.log(jax.random.uniform(ks[5], (N_A, SSM_HEADS), f32, 1.0, 16.0))
    ssm_d = 1.0 + nrm(ks[6], (N_A, SSM_HEADS), 0.1)
    ssm_norm_w = 1.0 + nrm(ks[7], (N_A, D_INNER), 0.1)
    ssm_out_w = nrm(ks[8], (N_A, D_INNER, D_MODEL), beta * D_INNER ** -0.5)

    kv_col_scale = jnp.concatenate([jnp.ones((ATT_DIM,), f32), jnp.full((ATT_DIM,), beta, f32),
                                    jnp.ones((ATT_HEADS,), f32)])
    kv_w = nrm(ks[9], (D_MODEL, 2 * ATT_DIM + ATT_HEADS), D_MODEL ** -0.5) * kv_col_scale
    kv_b_f = FORGET_BIAS_MEAN + nrm(ks[10], (ATT_HEADS,), 0.5)
    att_q_w = nrm(ks[11], (N_B, D_MODEL, ATT_DIM), D_MODEL ** -0.5)
    att_o_w = nrm(ks[12], (N_B, ATT_DIM, D_MODEL), beta * ATT_DIM ** -0.5)

    ffn_gate_w = nrm(ks[13], (DEPTH, D_MODEL, D_FF), D_MODEL ** -0.5)
    ffn_up_w = nrm(ks[14], (DEPTH, D_MODEL, D_FF), D_MODEL ** -0.5)
    ffn_down_w = nrm(ks[15], (DEPTH, D_FF, D_MODEL), beta * D_FF ** -0.5)

    ln_mix_g = 1.0 + nrm(ks[16], (DEPTH, D_MODEL), 0.1)
    ln_mix_b = nrm(ks[17], (DEPTH, D_MODEL), 0.02)
    ln_ffn_g = 1.0 + nrm(ks[18], (DEPTH, D_MODEL), 0.1)
    ln_ffn_b = nrm(ks[19], (DEPTH, D_MODEL), 0.02)

    return {"x": x, "ssm_in_w": ssm_in_w, "ssm_conv_w": ssm_conv_w, "ssm_conv_b": ssm_conv_b,
            "ssm_dt_bias": ssm_dt_bias, "ssm_a_log": ssm_a_log, "ssm_d": ssm_d,
            "ssm_norm_w": ssm_norm_w, "ssm_out_w": ssm_out_w, "kv_w": kv_w, "kv_b_f": kv_b_f,
            "att_q_w": att_q_w, "att_o_w": att_o_w, "ffn_gate_w": ffn_gate_w,
            "ffn_up_w": ffn_up_w, "ffn_down_w": ffn_down_w, "ln_mix_g": ln_mix_g,
            "ln_mix_b": ln_mix_b, "ln_ffn_g": ln_ffn_g, "ln_ffn_b": ln_ffn_b}


def reference(x, ssm_in_w, ssm_conv_w, ssm_conv_b, ssm_dt_bias, ssm_a_log, ssm_d, ssm_norm_w,
              ssm_out_w, kv_w, kv_b_f, att_q_w, att_o_w, ffn_gate_w, ffn_up_w, ffn_down_w,
              ln_mix_g, ln_mix_b, ln_ffn_g, ln_ffn_b):
    alpha = DEEPNORM_ALPHA
    shared = None
    for layer in range(DEPTH):
        if layer < N_A:
            i = layer
            h = _mamba2_mixer(x, ssm_in_w[i], ssm_conv_w[i], ssm_conv_b[i], ssm_dt_bias[i],
                              ssm_a_log[i], ssm_d[i], ssm_norm_w[i], ssm_out_w[i])
        else:
            j = layer - N_A
            k_sh, v_sh, fcum_sh = shared
            h = _forgetting_attention(x, att_q_w[j], att_o_w[j], k_sh, v_sh, fcum_sh)
        x = _layer_norm(alpha * x + h, ln_mix_g[layer], ln_mix_b[layer])
        x = _layer_norm(alpha * x + _swiglu(x, ffn_gate_w[layer], ffn_up_w[layer], ffn_down_w[layer]),
                        ln_ffn_g[layer], ln_ffn_b[layer])
        if layer == N_A - 1:
            shared = _shared_kv(x, kv_w, kv_b_f)
    return x
```

```python
import functools

import jax
import jax.numpy as jnp
from jax import lax
from jax.experimental import pallas as pl
from jax.experimental.pallas import tpu as pltpu

F32 = jnp.float32
BF16 = jnp.bfloat16

LANES = 128
VMEM_LIMIT = 56 * 1024 * 1024

SSM_HEAD_DIM = 64
SSM_GROUPS = 8
SSM_STATE = 128
CONV_K = 4
CHUNK = 64
ATT_HEAD_DIM = 64
LN_EPS = 1e-5
RMS_EPS = 1e-5

AUG0 = ATT_HEAD_DIM
AUG1 = ATT_HEAD_DIM + 3
ONE_LANE = ATT_HEAD_DIM


def _cparams(*sem):
    return pltpu.CompilerParams(dimension_semantics=sem, vmem_limit_bytes=VMEM_LIMIT)


def _const_spec(shape):
    nd = len(shape)
    return pl.BlockSpec(shape, lambda *_: (0,) * nd, pipeline_mode=pl.Buffered(1))


def _split3(a):
    hi = a.astype(BF16)
    r = a - hi.astype(F32)
    mid = r.astype(BF16)
    lo = (r - mid.astype(F32)).astype(BF16)
    return hi, mid, lo


def _dot(a, b):
    return jnp.dot(a, b, preferred_element_type=F32)


def _dot_nt(a, b):
    return lax.dot_general(a, b, (((1,), (1,)), ((), ())), preferred_element_type=F32)


def _dot_tn(a, b):
    return lax.dot_general(a, b, (((0,), (0,)), ((), ())), preferred_element_type=F32)


def _dot_f32_lhs(a, b_exact):
    hi, mid, lo = _split3(a)
    return _dot(hi, b_exact) + _dot(mid, b_exact) + _dot(lo, b_exact)


def _dot_f32_rhs(a_exact, b):
    hi, mid, lo = _split3(b)
    return _dot(a_exact, hi) + _dot(a_exact, mid) + _dot(a_exact, lo)


def _softplus(x):
    return jnp.maximum(x, 0.0) + jnp.log1p(jnp.exp(-jnp.abs(x)))


def _silu(x):
    return x * jax.nn.sigmoid(x)


def _layer_norm(t, g, b):
    mu = jnp.mean(t, axis=-1, keepdims=True)
    d = t - mu
    var = jnp.mean(d * d, axis=-1, keepdims=True)
    return d * lax.rsqrt(var + LN_EPS) * g + b


def _inproj_kernel(x_ref, w_ref, wdt_ref, zx_ref, dt_ref, *, tn):
    xb = x_ref[...].astype(BF16)
    n = w_ref.shape[1]
    for c in range(n // tn):
        zx_ref[:, c * tn:(c + 1) * tn] = _dot(xb, w_ref[:, c * tn:(c + 1) * tn])
    dt_ref[...] = _dot(xb, wdt_ref[...])


def _inproj(x2d, w, wdt, *, tm=512, tn=512):
    t, d = x2d.shape
    n = w.shape[1]
    return pl.pallas_call(
        functools.partial(_inproj_kernel, tn=tn),
        grid=(t // tm,),
        in_specs=[pl.BlockSpec((tm, d), lambda i: (i, 0)),
                  _const_spec((d, n)),
                  _const_spec((d, LANES))],
        out_specs=[pl.BlockSpec((tm, n), lambda i: (i, 0)),
                   pl.BlockSpec((tm, LANES), lambda i: (i, 0))],
        out_shape=[jax.ShapeDtypeStruct((t, n), F32),
                   jax.ShapeDtypeStruct((t, LANES), F32)],
        compiler_params=_cparams("parallel"),
        name="in_proj",
    )(x2d, w, wdt)


def _ssd_kernel(z_ref, xs_ref, bc_ref, dtr_ref, cw_ref, cb_ref, dtb_ref, alog_ref,
                dsk_ref, nw_ref, e_ref, tril_ref,
                y_ref,
                ubuf, xact, bact, cact, yacc, state_ref, *, rows, d_inner, n_bc):
    @pl.when(pl.program_id(1) == 0)
    def _():
        ubuf[0:8, :] = jnp.zeros((8, ubuf.shape[1]), F32)
        state_ref[...] = jnp.zeros(state_ref.shape, F32)

    ubuf[8:8 + rows, 0:d_inner] = xs_ref[...]
    ubuf[8:8 + rows, d_inner:] = bc_ref[...]

    cw = 512
    n_conv = ubuf.shape[1]
    for c in range(n_conv // cw):
        cs = slice(c * cw, (c + 1) * cw)
        acc = cb_ref[:, cs] + cw_ref[CONV_K - 1:CONV_K, cs] * ubuf[8:8 + rows, cs]
        for k in range(CONV_K - 1):
            off = 8 - (CONV_K - 1) + k
            acc = acc + cw_ref[k:k + 1, cs] * ubuf[off:off + rows, cs]
        act = _silu(acc)
        if c * cw < d_inner:
            xact[:, cs] = act
        elif c * cw < d_inner + n_bc:
            bact[:, c * cw - d_inner:(c + 1) * cw - d_inner] = act.astype(BF16)
        else:
            o = c * cw - d_inner - n_bc
            cact[:, o:o + cw] = act.astype(BF16)
    ubuf[0:8, :] = ubuf[rows:rows + 8, :]

    dt = _softplus(dtr_ref[...] + dtb_ref[...])
    a_neg = -jnp.exp(alog_ref[...])
    cum = _dot_f32_rhs(tril_ref[...], dt * a_neg)

    gw = (d_inner // SSM_GROUPS)
    hpg = gw // SSM_HEAD_DIM
    nch = rows // CHUNK

    li = lax.broadcasted_iota(jnp.int32, (CHUNK, gw), 0)
    si = lax.broadcasted_iota(jnp.int32, (CHUNK, gw), 1) % SSM_HEAD_DIM
    causal = li >= si
    diag = li == si
    bi = lax.broadcasted_iota(jnp.int32, (hpg * CHUNK, gw), 0) // CHUNK
    bj = lax.broadcasted_iota(jnp.int32, (hpg * CHUNK, gw), 1) // SSM_HEAD_DIM
    blockdiag = bi == bj

    for g in range(SSM_GROUPS):
        gs = slice(g * gw, (g + 1) * gw)
        ns = slice(g * SSM_STATE, (g + 1) * SSM_STATE)
        e_g = e_ref[:, gs]
        dtx = _dot_f32_lhs(dt, e_g)
        cumx = _dot_f32_lhs(cum, e_g)
        xs = xact[:, gs]
        xdt = xs * dtx
        ecum = jnp.exp(cumx)
        for c in range(nch):
            rs = slice(c * CHUNK, (c + 1) * CHUNK)
            cumx_c = cumx[rs]
            cum_end = cumx_c[CHUNK - 1:CHUNK, :]
            cg = cact[rs, ns]
            bg = bact[rs, ns]
            xdt_c = xdt[rs]
            st = state_ref[g]
            y_state = _dot(cg, st.astype(BF16)) * ecum[rs]
            cb4 = _dot_nt(cg, jnp.concatenate([bg] * hpg, axis=0))
            cum_row = jnp.sum(jnp.where(diag, cumx_c, 0.0), axis=0, keepdims=True)
            seg = jnp.where(causal, cumx_c - cum_row, -jnp.inf)
            w4 = (jnp.exp(seg) * cb4).astype(BF16)
            x4 = jnp.concatenate([xdt_c.astype(BF16)] * hpg, axis=0)
            x4 = jnp.where(blockdiag, x4, jnp.zeros_like(x4))
            yacc[rs, gs] = _dot(w4, x4) + y_state
            xw = (xdt_c * jnp.exp(cum_end - cumx_c)).astype(BF16)
            state_ref[g] = st * jnp.exp(cum_end) + _dot_tn(bg, xw)

    for g in range(SSM_GROUPS):
        gs = slice(g * gw, (g + 1) * gw)
        y = yacc[:, gs] + dsk_ref[:, gs] * xact[:, gs]
        y = y * _silu(z_ref[:, gs])
        ms = jnp.mean(y * y, axis=-1, keepdims=True)
        y_ref[:, gs] = (y * lax.rsqrt(ms + RMS_EPS) * nw_ref[:, gs]).astype(y_ref.dtype)


def _ssd(zx, dt_raw, conv_w, conv_b, dt_bias, a_log, d_skip, norm_w, expand, tril,
         *, batch, seq, d_inner, rows):
    t = zx.shape[0]
    n_conv = conv_w.shape[1]
    n_bc = (n_conv - d_inner) // 2
    nblk = seq // rows
    assert zx.shape[1] == d_inner + n_conv and n_conv == 2 * d_inner
    row_map = lambda col: (lambda b, s: (b * nblk + s, col))
    kern = functools.partial(_ssd_kernel, rows=rows, d_inner=d_inner, n_bc=n_bc)
    return pl.pallas_call(
        kern,
        grid=(batch, nblk),
        in_specs=[pl.BlockSpec((rows, d_inner), row_map(0)),
                  pl.BlockSpec((rows, d_inner), row_map(1)),
                  pl.BlockSpec((rows, d_inner), row_map(2)),
                  pl.BlockSpec((rows, LANES), row_map(0)),
                  _const_spec(conv_w.shape), _const_spec(conv_b.shape),
                  _const_spec(dt_bias.shape), _const_spec(a_log.shape),
                  _const_spec(d_skip.shape), _const_spec(norm_w.shape),
                  _const_spec(expand.shape), _const_spec(tril.shape)],
        out_specs=pl.BlockSpec((rows, d_inner), row_map(0)),
        out_shape=jax.ShapeDtypeStruct((t, d_inner), BF16),
        scratch_shapes=[pltpu.VMEM((rows + 8, n_conv), F32),
                        pltpu.VMEM((rows, d_inner), F32),
                        pltpu.VMEM((rows, n_bc), BF16),
                        pltpu.VMEM((rows, n_bc), BF16),
                        pltpu.VMEM((rows, d_inner), F32),
                        pltpu.VMEM((SSM_GROUPS, SSM_STATE, d_inner // SSM_GROUPS), F32)],
        compiler_params=_cparams("parallel", "arbitrary"),
        name="ssd_scan",
    )(zx, zx, zx, dt_raw, conv_w, conv_b, dt_bias, a_log, d_skip, norm_w, expand, tril)


def _proj_ln_kernel(a_ref, w_ref, x_ref, g_ref, b_ref, o_ref, *, alpha):
    h = _dot(a_ref[...], w_ref[...])
    o_ref[...] = _layer_norm(alpha * x_ref[...] + h, g_ref[...], b_ref[...])


def _proj_ln(a, w, x2d, g, b, *, alpha, tm=512):
    t, k = a.shape
    d = w.shape[1]
    return pl.pallas_call(
        functools.partial(_proj_ln_kernel, alpha=alpha),
        grid=(t // tm,),
        in_specs=[pl.BlockSpec((tm, k), lambda i: (i, 0)),
                  _const_spec((k, d)),
                  pl.BlockSpec((tm, d), lambda i: (i, 0)),
                  _const_spec((1, d)), _const_spec((1, d))],
        out_specs=pl.BlockSpec((tm, d), lambda i: (i, 0)),
        out_shape=jax.ShapeDtypeStruct((t, d), F32),
        compiler_params=_cparams("parallel"),
        name="proj_ln",
    )(a, w, x2d, g, b)


def _ffn_kernel(x_ref, wgu_ref, wd_ref, g_ref, b_ref, o_ref, a_scr, *, alpha, tf):
    x = x_ref[...]
    xb = x.astype(BF16)
    d_ff = wd_ref.shape[0]
    for c in range(d_ff // tf):
        gu = _dot(xb, wgu_ref[:, 2 * c * tf:2 * (c + 1) * tf])
        a_scr[:, c * tf:(c + 1) * tf] = (_silu(gu[:, :tf]) * gu[:, tf:]).astype(BF16)
    h = _dot(a_scr[...], wd_ref[...])
    o_ref[...] = _layer_norm(alpha * x + h, g_ref[...], b_ref[...])


def _ffn(x2d, wgu, wd, g, b, *, alpha, tm=512, tf=256):
    t, d = x2d.shape
    d_ff = wd.shape[0]
    assert d_ff % tf == 0
    return pl.pallas_call(
        functools.partial(_ffn_kernel, alpha=alpha, tf=tf),
        grid=(t // tm,),
        in_specs=[pl.BlockSpec((tm, d), lambda i: (i, 0)),
                  _const_spec((d, 2 * d_ff)),
                  _const_spec((d_ff, d)),
                  _const_spec((1, d)), _const_spec((1, d))],
        out_specs=pl.BlockSpec((tm, d), lambda i: (i, 0)),
        out_shape=jax.ShapeDtypeStruct((t, d), F32),
        scratch_shapes=[pltpu.VMEM((tm, d_ff), BF16)],
        compiler_params=_cparams("parallel"),
        name="ffn_ln",
    )(x2d, wgu, wd, g, b)


def _qkv_kernel(x_ref, w_ref, wf_ref, bf_ref, pq_ref, pk_ref, cq_ref, ck_ref, tril_ref,
                q_ref, k_ref, v_ref, carry, *, att_dim, n_heads, scale):
    @pl.when(pl.program_id(1) == 0)
    def _():
        carry[...] = jnp.zeros(carry.shape, F32)

    xb = x_ref[...].astype(BF16)
    tm = xb.shape[0]
    lane = lax.broadcasted_iota(jnp.int32, (tm, LANES), 1)

    logf = -_softplus(-(_dot(xb, wf_ref[...]) + bf_ref[...]))
    logf = jnp.where(lane < n_heads, logf, 0.0)
    fc = _dot_f32_rhs(tril_ref[...], logf) + carry[...]
    carry[...] = fc[tm - 1:tm, :]
    hi, mid, lo = _split3(fc)
    feats = (hi.astype(F32) + pltpu.roll(mid.astype(F32), n_heads, axis=1)
             + pltpu.roll(lo.astype(F32), 2 * n_heads, axis=1)).astype(BF16)
    q_aug = _dot(feats, pq_ref[...]) + cq_ref[...]
    k_aug = _dot(feats, pk_ref[...]) + ck_ref[...]

    low = lane < ATT_HEAD_DIM
    one_col = jnp.where(lane == ONE_LANE, 1.0, 0.0)
    tn = 2 * LANES
    for c in range(att_dim // tn):
        q4 = _dot(xb, w_ref[:, c * tn:(c + 1) * tn]) * scale
        k4 = _dot(xb, w_ref[:, att_dim + c * tn:att_dim + (c + 1) * tn])
        v4 = _dot(xb, w_ref[:, 2 * att_dim + c * tn:2 * att_dim + (c + 1) * tn])
        for h in range(tn // ATT_HEAD_DIM):
            ps = slice((h // 2) * LANES, (h // 2 + 1) * LANES)
            qh, kh, vh = q4[:, ps], k4[:, ps], v4[:, ps]
            if h % 2:
                qh, kh, vh = (pltpu.roll(a, ATT_HEAD_DIM, axis=1) for a in (qh, kh, vh))
            head = c * (tn // ATT_HEAD_DIM) + h
            hs = slice(head * LANES, (head + 1) * LANES)
            q_ref[:, hs] = jnp.where(low, qh, q_aug[:, hs]).astype(BF16)
            k_ref[:, hs] = jnp.where(low, kh, k_aug[:, hs]).astype(BF16)
            v_ref[:, hs] = jnp.where(low, vh, one_col).astype(BF16)


def _qkv(x2d, w, wf, bf, pq, pk, cq, ck, tril, *, batch, seq, att_dim, n_heads, scale, tm):
    t, d = x2d.shape
    nblk = seq // tm
    wide = n_heads * LANES
    row = lambda b, s: (b * nblk + s, 0)
    kern = functools.partial(_qkv_kernel, att_dim=att_dim, n_heads=n_heads, scale=scale)
    return pl.pallas_call(
        kern,
        grid=(batch, nblk),
        in_specs=[pl.BlockSpec((tm, d), row),
                  _const_spec(w.shape), _const_spec(wf.shape), _const_spec(bf.shape),
                  _const_spec(pq.shape), _const_spec(pk.shape),
                  _const_spec(cq.shape), _const_spec(ck.shape), _const_spec(tril.shape)],
        out_specs=[pl.BlockSpec((tm, wide), row)] * 3,
        out_shape=[jax.ShapeDtypeStruct((t, wide), BF16)] * 3,
        scratch_shapes=[pltpu.VMEM((1, LANES), F32)],
        compiler_params=_cparams("parallel", "arbitrary"),
        name="qkv_proj",
    )(x2d, w, wf, bf, pq, pk, cq, ck, tril)


def _attn_kernel(q_ref, k_ref, v_ref, o_ref, *, tq):
    qi = pl.program_id(2)
    row = lax.broadcasted_iota(jnp.int32, (tq, tq), 0)
    col = lax.broadcasted_iota(jnp.int32, (tq, tq), 1)
    tri = col <= row
    lane = lax.broadcasted_iota(jnp.int32, (tq, LANES), 1)
    outs = []
    for hh in range(2):
        hs = slice(hh * LANES, (hh + 1) * LANES)
        q = q_ref[0, :, hs]

        def step(j, carry, masked):
            m, acc = carry
            off = pl.multiple_of(j * tq, tq)
            k = k_ref[0, pl.ds(off, tq), hs]
            v = v_ref[0, pl.ds(off, tq), hs]
            s = _dot_nt(q, k)
            if masked:
                s = jnp.where(tri, s, -jnp.inf)
            m_new = jnp.maximum(m, jnp.max(s, axis=1, keepdims=True))
            p = jnp.exp(s - m_new)
            acc = jnp.exp(m - m_new) * acc + _dot(p.astype(BF16), v)
            return m_new, acc

        carry = (jnp.full((tq, 1), -jnp.inf, F32), jnp.zeros((tq, LANES), F32))
        carry = lax.fori_loop(0, qi, functools.partial(step, masked=False), carry)
        _, acc = step(qi, carry, True)
        outs.append(acc / acc[:, ONE_LANE:ONE_LANE + 1])
    o = jnp.where(lane < ATT_HEAD_DIM, outs[0], pltpu.roll(outs[1], ATT_HEAD_DIM, axis=1))
    o_ref[0] = o.astype(o_ref.dtype)


def _attention(q, k, v, *, batch, seq, n_heads, tq):
    wide = n_heads * LANES
    q3, k3, v3 = (a.reshape(batch, seq, wide) for a in (q, k, v))
    return pl.pallas_call(
        functools.partial(_attn_kernel, tq=tq),
        grid=(batch, n_heads // 2, seq // tq),
        in_specs=[pl.BlockSpec((1, tq, 2 * LANES), lambda b, p, i: (b, i, p)),
                  pl.BlockSpec((1, seq, 2 * LANES), lambda b, p, i: (b, 0, p)),
                  pl.BlockSpec((1, seq, 2 * LANES), lambda b, p, i: (b, 0, p))],
        out_specs=pl.BlockSpec((1, tq, LANES), lambda b, p, i: (b, i, p)),
        out_shape=jax.ShapeDtypeStruct((batch, seq, n_heads * ATT_HEAD_DIM), BF16),
        compiler_params=_cparams("parallel", "parallel", "arbitrary"),
        name="fox_attention",
    )(q3, k3, v3)


def _pad_lanes(a, n=LANES):
    return jnp.pad(a, [(0, 0)] * (a.ndim - 1) + [(0, n - a.shape[-1])])


def _block_tril(n, block):
    i = jnp.arange(n)
    return ((i[:, None] >= i[None, :]) & (i[:, None] // block == i[None, :] // block)).astype(BF16)


def _interleave_gate_up(wg, wu, tf):
    d, d_ff = wg.shape
    w = jnp.stack([wg.reshape(d, d_ff // tf, tf), wu.reshape(d, d_ff // tf, tf)], axis=2)
    return w.reshape(d, 2 * d_ff)


def kernel(x, ssm_in_w, ssm_conv_w, ssm_conv_b, ssm_dt_bias, ssm_a_log, ssm_d, ssm_norm_w, ssm_out_w,
           kv_w, kv_b_f, att_q_w, att_o_w, ffn_gate_w, ffn_up_w, ffn_down_w,
           ln_mix_g, ln_mix_b, ln_ffn_g, ln_ffn_b):
    batch, seq, d_model = x.shape
    depth = ffn_gate_w.shape[0]
    assert ssm_in_w.shape[0] == 1 and att_q_w.shape[0] == 1 and depth == 2
    alpha = (2.0 * depth) ** 0.25
    d_inner = ssm_out_w.shape[1]
    n_conv = ssm_conv_w.shape[2]
    n_ssm_heads = ssm_dt_bias.shape[1]
    att_dim = att_q_w.shape[2]
    n_att_heads = att_dim // ATT_HEAD_DIM
    tf = 256
    ssd_rows = 256
    t = batch * seq
    x2d = x.reshape(t, d_model)
    row2 = lambda a: a.reshape(1, -1).astype(F32)

    in_w = ssm_in_w[0]
    n_main = d_inner + n_conv
    zx, dt_raw = _inproj(x2d, in_w[:, :n_main].astype(BF16), _pad_lanes(in_w[:, n_main:]).astype(BF16))
    head_of_lane = jnp.arange(d_inner) // SSM_HEAD_DIM
    expand = (jnp.arange(LANES)[:, None] == head_of_lane[None, :]).astype(BF16)
    y = _ssd(zx, dt_raw, ssm_conv_w[0], row2(ssm_conv_b[0]),
             _pad_lanes(row2(ssm_dt_bias[0])), _pad_lanes(row2(ssm_a_log[0])),
             row2(jnp.repeat(ssm_d[0], SSM_HEAD_DIM)), row2(ssm_norm_w[0]),
             expand, _block_tril(ssd_rows, CHUNK),
             batch=batch, seq=seq, d_inner=d_inner, rows=ssd_rows)
    x2d = _proj_ln(y, ssm_out_w[0].astype(BF16), x2d, row2(ln_mix_g[0]), row2(ln_mix_b[0]), alpha=alpha)
    x2d = _ffn(x2d, _interleave_gate_up(ffn_gate_w[0], ffn_up_w[0], tf).astype(BF16),
               ffn_down_w[0].astype(BF16), row2(ln_ffn_g[0]), row2(ln_ffn_b[0]), alpha=alpha, tf=tf)

    wqkv = jnp.concatenate([att_q_w[0], kv_w[:, :2 * att_dim]], axis=1).astype(BF16)
    wf = _pad_lanes(kv_w[:, 2 * att_dim:]).astype(BF16)
    wide = n_att_heads * LANES
    feat = jnp.arange(LANES)[:, None]
    slot = jnp.arange(wide)[None, :]
    place = lambda base, sign: sum(
        sign * ((feat == part * n_att_heads + h) & (slot == h * LANES + base + part)).astype(F32)
        for part in range(3) for h in range(n_att_heads))
    in_slot = lambda base: sum((slot == h * LANES + base + part).astype(F32)
                               for part in range(3) for h in range(n_att_heads))
    pq, pk = place(AUG0, 1.0).astype(BF16), place(AUG1, -1.0).astype(BF16)
    cq, ck = in_slot(AUG1), in_slot(AUG0)
    tm_qkv = 512
    q, k, v = _qkv(x2d, wqkv, wf, _pad_lanes(row2(kv_b_f)), pq, pk, cq, ck, _block_tril(tm_qkv, tm_qkv),
                   batch=batch, seq=seq, att_dim=att_dim, n_heads=n_att_heads,
                   scale=ATT_HEAD_DIM ** -0.5, tm=tm_qkv)

    o = _attention(q, k, v, batch=batch, seq=seq, n_heads=n_att_heads, tq=256)
    x2d = _proj_ln(o.reshape(t, att_dim), att_o_w[0].astype(BF16), x2d,
                   row2(ln_mix_g[1]), row2(ln_mix_b[1]), alpha=alpha)
    x2d = _ffn(x2d, _interleave_gate_up(ffn_gate_w[1], ffn_up_w[1], tf).astype(BF16),
               ffn_down_w[1].astype(BF16), row2(ln_ffn_g[1]), row2(ln_ffn_b[1]), alpha=alpha, tf=tf)
    return x2d.reshape(batch, seq, d_model)
```

```python
import functools

import jax
import jax.numpy as jnp
from jax import lax
from jax.experimental import pallas as pl
from jax.experimental.pallas import tpu as pltpu

F32 = jnp.float32
BF16 = jnp.bfloat16

LANES = 128
VMEM_LIMIT = 56 * 1024 * 1024

SSM_HEAD_DIM = 64
SSM_GROUPS = 8
SSM_STATE = 128
CONV_K = 4
CHUNK = 64
ATT_HEAD_DIM = 64
LN_EPS = 1e-5
RMS_EPS = 1e-5

AUG0 = ATT_HEAD_DIM
AUG1 = ATT_HEAD_DIM + 3
SUM_ROWS = 16


def _cparams(*sem):
    return pltpu.CompilerParams(dimension_semantics=sem, vmem_limit_bytes=VMEM_LIMIT)


def _const_spec(shape):
    nd = len(shape)
    return pl.BlockSpec(shape, lambda *_: (0,) * nd, pipeline_mode=pl.Buffered(1))


def _split3(a):
    hi = a.astype(BF16)
    r = a - hi.astype(F32)
    mid = r.astype(BF16)
    lo = (r - mid.astype(F32)).astype(BF16)
    return hi, mid, lo


def _dot(a, b):
    return jnp.dot(a, b, preferred_element_type=F32)


def _dot_nt(a, b):
    return lax.dot_general(a, b, (((1,), (1,)), ((), ())), preferred_element_type=F32)


def _dot_tn(a, b):
    return lax.dot_general(a, b, (((0,), (0,)), ((), ())), preferred_element_type=F32)


def _dot_f32_lhs(a, b_exact):
    hi, mid, lo = _split3(a)
    return _dot(hi, b_exact) + _dot(mid, b_exact) + _dot(lo, b_exact)


def _dot_f32_rhs(a_exact, b):
    hi, mid, lo = _split3(b)
    return _dot(a_exact, hi) + _dot(a_exact, mid) + _dot(a_exact, lo)


def _softplus(x):
    return jnp.maximum(x, 0.0) + jnp.log1p(jnp.exp(-jnp.abs(x)))


def _silu(x):
    return x * jax.nn.sigmoid(x)


def _layer_norm(t, g, b):
    mu = jnp.mean(t, axis=-1, keepdims=True)
    d = t - mu
    var = jnp.mean(d * d, axis=-1, keepdims=True)
    return d * lax.rsqrt(var + LN_EPS) * g + b


def _inproj_kernel(x_ref, w_ref, wdt_ref, zx_ref, dt_ref, *, tn):
    xb = x_ref[...].astype(BF16)
    n = w_ref.shape[1]
    for c in range(n // tn):
        zx_ref[:, c * tn:(c + 1) * tn] = _dot(xb, w_ref[:, c * tn:(c + 1) * tn])
    dt_ref[...] = _dot(xb, wdt_ref[...])


def _inproj(x2d, w, wdt, *, tm=512, tn=512):
    t, d = x2d.shape
    n = w.shape[1]
    return pl.pallas_call(
        functools.partial(_inproj_kernel, tn=tn),
        grid=(t // tm,),
        in_specs=[pl.BlockSpec((tm, d), lambda i: (i, 0)),
                  _const_spec((d, n)),
                  _const_spec((d, LANES))],
        out_specs=[pl.BlockSpec((tm, n), lambda i: (i, 0)),
                   pl.BlockSpec((tm, LANES), lambda i: (i, 0))],
        out_shape=[jax.ShapeDtypeStruct((t, n), F32),
                   jax.ShapeDtypeStruct((t, LANES), F32)],
        compiler_params=_cparams("parallel"),
        name="in_proj",
    )(x2d, w, wdt)


def _ssd_kernel(z_ref, xs_ref, bc_ref, dtr_ref, cw_ref, cb_ref, dtb_ref, alog_ref,
                dsk_ref, nw_ref, e_ref, tril_ref,
                y_ref,
                ubuf, xact, bact, cact, yacc, state_ref, *, rows, d_inner, n_bc):
    @pl.when(pl.program_id(1) == 0)
    def _():
        ubuf[0:8, :] = jnp.zeros((8, ubuf.shape[1]), F32)
        state_ref[...] = jnp.zeros(state_ref.shape, F32)

    ubuf[8:8 + rows, 0:d_inner] = xs_ref[...]
    ubuf[8:8 + rows, d_inner:] = bc_ref[...]

    cw = 512
    n_conv = ubuf.shape[1]
    for c in range(n_conv // cw):
        cs = slice(c * cw, (c + 1) * cw)
        acc = cb_ref[:, cs] + cw_ref[CONV_K - 1:CONV_K, cs] * ubuf[8:8 + rows, cs]
        for k in range(CONV_K - 1):
            off = 8 - (CONV_K - 1) + k
            acc = acc + cw_ref[k:k + 1, cs] * ubuf[off:off + rows, cs]
        act = _silu(acc)
        if c * cw < d_inner:
            xact[:, cs] = act
        elif c * cw < d_inner + n_bc:
            bact[:, c * cw - d_inner:(c + 1) * cw - d_inner] = act.astype(BF16)
        else:
            o = c * cw - d_inner - n_bc
            cact[:, o:o + cw] = act.astype(BF16)
    ubuf[0:8, :] = ubuf[rows:rows + 8, :]

    dt = _softplus(dtr_ref[...] + dtb_ref[...])
    a_neg = -jnp.exp(alog_ref[...])
    cum = _dot_f32_rhs(tril_ref[...], dt * a_neg)

    gw = (d_inner // SSM_GROUPS)
    hpg = gw // SSM_HEAD_DIM
    nch = rows // CHUNK

    li = lax.broadcasted_iota(jnp.int32, (CHUNK, gw), 0)
    si = lax.broadcasted_iota(jnp.int32, (CHUNK, gw), 1) % SSM_HEAD_DIM
    causal = li >= si
    diag = li == si
    bi = lax.broadcasted_iota(jnp.int32, (hpg * CHUNK, gw), 0) // CHUNK
    bj = lax.broadcasted_iota(jnp.int32, (hpg * CHUNK, gw), 1) // SSM_HEAD_DIM
    blockdiag = bi == bj

    for g in range(SSM_GROUPS):
        gs = slice(g * gw, (g + 1) * gw)
        ns = slice(g * SSM_STATE, (g + 1) * SSM_STATE)
        e_g = e_ref[:, gs]
        dtx = _dot_f32_lhs(dt, e_g)
        cumx = _dot_f32_lhs(cum, e_g)
        xs = xact[:, gs]
        xdt = xs * dtx
        ecum = jnp.exp(cumx)
        for c in range(nch):
            rs = slice(c * CHUNK, (c + 1) * CHUNK)
            cumx_c = cumx[rs]
            cum_end = cumx_c[CHUNK - 1:CHUNK, :]
            cg = cact[rs, ns]
            bg = bact[rs, ns]
            xdt_c = xdt[rs]
            st = state_ref[g]
            y_state = _dot(cg, st.astype(BF16)) * ecum[rs]
            cb4 = _dot_nt(cg, jnp.concatenate([bg] * hpg, axis=0))
            cum_row = jnp.sum(jnp.where(diag, cumx_c, 0.0), axis=0, keepdims=True)
            seg = jnp.where(causal, cumx_c - cum_row, -jnp.inf)
            w4 = (jnp.exp(seg) * cb4).astype(BF16)
            x4 = jnp.concatenate([xdt_c.astype(BF16)] * hpg, axis=0)
            x4 = jnp.where(blockdiag, x4, jnp.zeros_like(x4))
            yacc[rs, gs] = _dot(w4, x4) + y_state
            xw = (xdt_c * jnp.exp(cum_end - cumx_c)).astype(BF16)
            state_ref[g] = st * jnp.exp(cum_end) + _dot_tn(bg, xw)

    for g in range(SSM_GROUPS):
        gs = slice(g * gw, (g + 1) * gw)
        y = yacc[:, gs] + dsk_ref[:, gs] * xact[:, gs]
        y = y * _silu(z_ref[:, gs])
        ms = jnp.mean(y * y, axis=-1, keepdims=True)
        y_ref[:, gs] = (y * lax.rsqrt(ms + RMS_EPS) * nw_ref[:, gs]).astype(y_ref.dtype)


def _ssd(zx, dt_raw, conv_w, conv_b, dt_bias, a_log, d_skip, norm_w, expand, tril,
         *, batch, seq, d_inner, rows):
    t = zx.shape[0]
    n_conv = conv_w.shape[1]
    n_bc = (n_conv - d_inner) // 2
    nblk = seq // rows
    assert zx.shape[1] == d_inner + n_conv and n_conv == 2 * d_inner
    row_map = lambda col: (lambda b, s: (b * nblk + s, col))
    kern = functools.partial(_ssd_kernel, rows=rows, d_inner=d_inner, n_bc=n_bc)
    return pl.pallas_call(
        kern,
        grid=(batch, nblk),
        in_specs=[pl.BlockSpec((rows, d_inner), row_map(0)),
                  pl.BlockSpec((rows, d_inner), row_map(1)),
                  pl.BlockSpec((rows, d_inner), row_map(2)),
                  pl.BlockSpec((rows, LANES), row_map(0)),
                  _const_spec(conv_w.shape), _const_spec(conv_b.shape),
                  _const_spec(dt_bias.shape), _const_spec(a_log.shape),
                  _const_spec(d_skip.shape), _const_spec(norm_w.shape),
                  _const_spec(expand.shape), _const_spec(tril.shape)],
        out_specs=pl.BlockSpec((rows, d_inner), row_map(0)),
        out_shape=jax.ShapeDtypeStruct((t, d_inner), BF16),
        scratch_shapes=[pltpu.VMEM((rows + 8, n_conv), F32),
                        pltpu.VMEM((rows, d_inner), F32),
                        pltpu.VMEM((rows, n_bc), BF16),
                        pltpu.VMEM((rows, n_bc), BF16),
                        pltpu.VMEM((rows, d_inner), F32),
                        pltpu.VMEM((SSM_GROUPS, SSM_STATE, d_inner // SSM_GROUPS), F32)],
        compiler_params=_cparams("parallel", "arbitrary"),
        name="ssd_scan",
    )(zx, zx, zx, dt_raw, conv_w, conv_b, dt_bias, a_log, d_skip, norm_w, expand, tril)


def _proj_ln_kernel(a_ref, w_ref, x_ref, g_ref, b_ref, o_ref, *, alpha):
    h = _dot(a_ref[...], w_ref[...])
    o_ref[...] = _layer_norm(alpha * x_ref[...] + h, g_ref[...], b_ref[...])


def _proj_ln(a, w, x2d, g, b, *, alpha, tm=512):
    t, k = a.shape
    d = w.shape[1]
    return pl.pallas_call(
        functools.partial(_proj_ln_kernel, alpha=alpha),
        grid=(t // tm,),
        in_specs=[pl.BlockSpec((tm, k), lambda i: (i, 0)),
                  _const_spec((k, d)),
                  pl.BlockSpec((tm, d), lambda i: (i, 0)),
                  _const_spec((1, d)), _const_spec((1, d))],
        out_specs=pl.BlockSpec((tm, d), lambda i: (i, 0)),
        out_shape=jax.ShapeDtypeStruct((t, d), F32),
        compiler_params=_cparams("parallel"),
        name="proj_ln",
    )(a, w, x2d, g, b)


def _ffn_kernel(x_ref, wgu_ref, wd_ref, g_ref, b_ref, o_ref, a_scr, *, alpha, tf):
    x = x_ref[...]
    xb = x.astype(BF16)
    d_ff = wd_ref.shape[0]
    for c in range(d_ff // tf):
        gu = _dot(xb, wgu_ref[:, 2 * c * tf:2 * (c + 1) * tf])
        a_scr[:, c * tf:(c + 1) * tf] = (_silu(gu[:, :tf]) * gu[:, tf:]).astype(BF16)
    h = _dot(a_scr[...], wd_ref[...])
    o_ref[...] = _layer_norm(alpha * x + h, g_ref[...], b_ref[...])


def _ffn(x2d, wgu, wd, g, b, *, alpha, tm=512, tf=256):
    t, d = x2d.shape
    d_ff = wd.shape[0]
    assert d_ff % tf == 0
    return pl.pallas_call(
        functools.partial(_ffn_kernel, alpha=alpha, tf=tf),
        grid=(t // tm,),
        in_specs=[pl.BlockSpec((tm, d), lambda i: (i, 0)),
                  _const_spec((d, 2 * d_ff)),
                  _const_spec((d_ff, d)),
                  _const_spec((1, d)), _const_spec((1, d))],
        out_specs=pl.BlockSpec((tm, d), lambda i: (i, 0)),
        out_shape=jax.ShapeDtypeStruct((t, d), F32),
        scratch_shapes=[pltpu.VMEM((tm, d_ff), BF16)],
        compiler_params=_cparams("parallel"),
        name="ffn_ln",
    )(x2d, wgu, wd, g, b)


def _qkv_kernel(x_ref, w_ref, wvt_ref, wf_ref, bf_ref, pq_ref, pk_ref, cq_ref, ck_ref, tril_ref,
                q_ref, k_ref, vt_ref, carry, *, att_dim, n_heads, scale):
    @pl.when(pl.program_id(1) == 0)
    def _():
        carry[...] = jnp.zeros(carry.shape, F32)

    xb = x_ref[...].astype(BF16)
    tm = xb.shape[0]
    lane = lax.broadcasted_iota(jnp.int32, (tm, LANES), 1)

    logf = -_softplus(-(_dot(xb, wf_ref[...]) + bf_ref[...]))
    logf = jnp.where(lane < n_heads, logf, 0.0)
    fc = _dot_f32_rhs(tril_ref[...], logf) + carry[...]
    carry[...] = fc[tm - 1:tm, :]
    hi, mid, lo = _split3(fc)
    feats = (hi.astype(F32) + pltpu.roll(mid.astype(F32), n_heads, axis=1)
             + pltpu.roll(lo.astype(F32), 2 * n_heads, axis=1)).astype(BF16)
    q_aug = _dot(feats, pq_ref[...]) + cq_ref[...]
    k_aug = _dot(feats, pk_ref[...]) + ck_ref[...]

    low = lane < ATT_HEAD_DIM
    tn = 2 * LANES
    for c in range(att_dim // tn):
        q4 = _dot(xb, w_ref[:, c * tn:(c + 1) * tn]) * scale
        k4 = _dot(xb, w_ref[:, att_dim + c * tn:att_dim + (c + 1) * tn])
        for h in range(tn // ATT_HEAD_DIM):
            ps = slice((h // 2) * LANES, (h // 2 + 1) * LANES)
            qh, kh = q4[:, ps], k4[:, ps]
            if h % 2:
                qh, kh = (pltpu.roll(a, ATT_HEAD_DIM, axis=1) for a in (qh, kh))
            head = c * (tn // ATT_HEAD_DIM) + h
            hs = slice(head * LANES, (head + 1) * LANES)
            q_ref[:, hs] = jnp.where(low, qh, q_aug[:, hs]).astype(BF16)
            k_ref[:, hs] = jnp.where(low, kh, k_aug[:, hs]).astype(BF16)
    vt_ref[0] = _dot_nt(wvt_ref[...], xb).astype(BF16)


def _qkv(x2d, w, wvt, wf, bf, pq, pk, cq, ck, tril, *, batch, seq, att_dim, n_heads, scale, tm):
    t, d = x2d.shape
    nblk = seq // tm
    wide = n_heads * LANES
    row = lambda b, s: (b * nblk + s, 0)
    kern = functools.partial(_qkv_kernel, att_dim=att_dim, n_heads=n_heads, scale=scale)
    return pl.pallas_call(
        kern,
        grid=(batch, nblk),
        in_specs=[pl.BlockSpec((tm, d), row),
                  _const_spec(w.shape), _const_spec(wvt.shape), _const_spec(wf.shape), _const_spec(bf.shape),
                  _const_spec(pq.shape), _const_spec(pk.shape),
                  _const_spec(cq.shape), _const_spec(ck.shape), _const_spec(tril.shape)],
        out_specs=[pl.BlockSpec((tm, wide), row), pl.BlockSpec((tm, wide), row),
                   pl.BlockSpec((1, att_dim, tm), lambda b, s: (b, 0, s))],
        out_shape=[jax.ShapeDtypeStruct((t, wide), BF16), jax.ShapeDtypeStruct((t, wide), BF16),
                   jax.ShapeDtypeStruct((batch, att_dim, seq), BF16)],
        scratch_shapes=[pltpu.VMEM((1, LANES), F32)],
        compiler_params=_cparams("parallel", "arbitrary"),
        name="qkv_proj",
    )(x2d, w, wvt, wf, bf, pq, pk, cq, ck, tril)


def _attn_kernel(q_ref, k_ref, vt_ref, o_ref, vaug, *, tq, seq):
    d = ATT_HEAD_DIM
    ones = jnp.ones((SUM_ROWS, seq), BF16)
    for hh in range(2):
        vaug[hh, 0:d, :] = vt_ref[0, hh * d:(hh + 1) * d, :]
        vaug[hh, d:d + SUM_ROWS, :] = ones
    row = lax.broadcasted_iota(jnp.int32, (tq, tq), 0)
    col = lax.broadcasted_iota(jnp.int32, (tq, tq), 1)
    causal = row <= col
    for qi in range(seq // tq):
        qs = slice(qi * tq, (qi + 1) * tq)
        outs = []
        for hh in range(2):
            hs = slice(hh * LANES, (hh + 1) * LANES)
            q = q_ref[0, qs, hs]
            m = acc = None
            for j in range(qi + 1):
                ks = slice(j * tq, (j + 1) * tq)
                s = _dot_nt(k_ref[0, ks, hs], q)
                if j == qi:
                    s = jnp.where(causal, s, -jnp.inf)
                cm = jnp.max(s, axis=0, keepdims=True)
                m_new = cm if m is None else jnp.maximum(m, cm)
                pv = _dot(vaug[hh, :, ks], jnp.exp(s - m_new).astype(BF16))
                acc = pv if m is None else jnp.exp(m - m_new) * acc + pv
                m = m_new
            outs.append(acc[0:d] / acc[d:d + 1])
        o_ref[0, qs, :] = jnp.concatenate(outs, axis=0).T.astype(o_ref.dtype)


def _attention(q, k, vt, *, batch, seq, n_heads, tq):
    wide = n_heads * LANES
    q3, k3 = (a.reshape(batch, seq, wide) for a in (q, k))
    pair = lambda b, p: (b, 0, p)
    return pl.pallas_call(
        functools.partial(_attn_kernel, tq=tq, seq=seq),
        grid=(batch, n_heads // 2),
        in_specs=[pl.BlockSpec((1, seq, 2 * LANES), pair),
                  pl.BlockSpec((1, seq, 2 * LANES), pair),
                  pl.BlockSpec((1, 2 * ATT_HEAD_DIM, seq), lambda b, p: (b, p, 0))],
        out_specs=pl.BlockSpec((1, seq, LANES), pair),
        out_shape=jax.ShapeDtypeStruct((batch, seq, n_heads * ATT_HEAD_DIM), BF16),
        scratch_shapes=[pltpu.VMEM((2, ATT_HEAD_DIM + SUM_ROWS, seq), BF16)],
        compiler_params=_cparams("parallel", "parallel"),
        name="fox_attention",
    )(q3, k3, vt)


def _pad_lanes(a, n=LANES):
    return jnp.pad(a, [(0, 0)] * (a.ndim - 1) + [(0, n - a.shape[-1])])


def _block_tril(n, block):
    i = jnp.arange(n)
    return ((i[:, None] >= i[None, :]) & (i[:, None] // block == i[None, :] // block)).astype(BF16)


def _interleave_gate_up(wg, wu, tf):
    d, d_ff = wg.shape
    w = jnp.stack([wg.reshape(d, d_ff // tf, tf), wu.reshape(d, d_ff // tf, tf)], axis=2)
    return w.reshape(d, 2 * d_ff)


def kernel(x, ssm_in_w, ssm_conv_w, ssm_conv_b, ssm_dt_bias, ssm_a_log, ssm_d, ssm_norm_w, ssm_out_w,
           kv_w, kv_b_f, att_q_w, att_o_w, ffn_gate_w, ffn_up_w, ffn_down_w,
           ln_mix_g, ln_mix_b, ln_ffn_g, ln_ffn_b):
    batch, seq, d_model = x.shape
    depth = ffn_gate_w.shape[0]
    assert ssm_in_w.shape[0] == 1 and att_q_w.shape[0] == 1 and depth == 2
    alpha = (2.0 * depth) ** 0.25
    d_inner = ssm_out_w.shape[1]
    n_conv = ssm_conv_w.shape[2]
    n_ssm_heads = ssm_dt_bias.shape[1]
    att_dim = att_q_w.shape[2]
    n_att_heads = att_dim // ATT_HEAD_DIM
    tf = 256
    ssd_rows = 256
    t = batch * seq
    x2d = x.reshape(t, d_model)
    row2 = lambda a: a.reshape(1, -1).astype(F32)

    in_w = ssm_in_w[0]
    n_main = d_inner + n_conv
    zx, dt_raw = _inproj(x2d, in_w[:, :n_main].astype(BF16), _pad_lanes(in_w[:, n_main:]).astype(BF16))
    head_of_lane = jnp.arange(d_inner) // SSM_HEAD_DIM
    expand = (jnp.arange(LANES)[:, None] == head_of_lane[None, :]).astype(BF16)
    y = _ssd(zx, dt_raw, ssm_conv_w[0], row2(ssm_conv_b[0]),
             _pad_lanes(row2(ssm_dt_bias[0])), _pad_lanes(row2(ssm_a_log[0])),
             row2(jnp.repeat(ssm_d[0], SSM_HEAD_DIM)), row2(ssm_norm_w[0]),
             expand, _block_tril(ssd_rows, CHUNK),
             batch=batch, seq=seq, d_inner=d_inner, rows=ssd_rows)
    x2d = _proj_ln(y, ssm_out_w[0].astype(BF16), x2d, row2(ln_mix_g[0]), row2(ln_mix_b[0]), alpha=alpha)
    x2d = _ffn(x2d, _interleave_gate_up(ffn_gate_w[0], ffn_up_w[0], tf).astype(BF16),
               ffn_down_w[0].astype(BF16), row2(ln_ffn_g[0]), row2(ln_ffn_b[0]), alpha=alpha, tf=tf)

    wqk = jnp.concatenate([att_q_w[0], kv_w[:, :att_dim]], axis=1).astype(BF16)
    wvt = kv_w[:, att_dim:2 * att_dim].T.astype(BF16)
    wf = _pad_lanes(kv_w[:, 2 * att_dim:]).astype(BF16)
    wide = n_att_heads * LANES
    feat = jnp.arange(LANES)[:, None]
    slot = jnp.arange(wide)[None, :]
    place = lambda base, sign: sum(
        sign * ((feat == part * n_att_heads + h) & (slot == h * LANES + base + part)).astype(F32)
        for part in range(3) for h in range(n_att_heads))
    in_slot = lambda base: sum((slot == h * LANES + base + part).astype(F32)
                               for part in range(3) for h in range(n_att_heads))
    pq, pk = place(AUG0, 1.0).astype(BF16), place(AUG1, -1.0).astype(BF16)
    cq, ck = in_slot(AUG1), in_slot(AUG0)
    tm_qkv = 512
    q, k, vt = _qkv(x2d, wqk, wvt, wf, _pad_lanes(row2(kv_b_f)), pq, pk, cq, ck,
                    _block_tril(tm_qkv, tm_qkv),
                    batch=batch, seq=seq, att_dim=att_dim, n_heads=n_att_heads,
                    scale=ATT_HEAD_DIM ** -0.5, tm=tm_qkv)

    o = _attention(q, k, vt, batch=batch, seq=seq, n_heads=n_att_heads, tq=512)
    x2d = _proj_ln(o.reshape(t, att_dim), att_o_w[0].astype(BF16), x2d,
                   row2(ln_mix_g[1]), row2(ln_mix_b[1]), alpha=alpha)
    x2d = _ffn(x2d, _interleave_gate_up(ffn_gate_w[1], ffn_up_w[1], tf).astype(BF16),
               ffn_down_w[1].astype(BF16), row2(ln_ffn_g[1]), row2(ln_ffn_b[1]), alpha=alpha, tf=tf)
    return x2d.reshape(batch, seq, d_model)
```

```python
import functools

import jax
import jax.numpy as jnp
from jax import lax
from jax.experimental import pallas as pl
from jax.experimental.pallas import tpu as pltpu

F32 = jnp.float32
BF16 = jnp.bfloat16

LANES = 128
VMEM_LIMIT = 56 * 1024 * 1024

SSM_HEAD_DIM = 64
SSM_GROUPS = 8
SSM_STATE = 128
CONV_K = 4
CHUNK = 64
ATT_HEAD_DIM = 64
LN_EPS = 1e-5
RMS_EPS = 1e-5

AUG0 = ATT_HEAD_DIM
AUG1 = ATT_HEAD_DIM + 3
SUM_ROWS = 16


def _cparams(*sem):
    return pltpu.CompilerParams(dimension_semantics=sem, vmem_limit_bytes=VMEM_LIMIT)


def _const_spec(shape):
    nd = len(shape)
    return pl.BlockSpec(shape, lambda *_: (0,) * nd, pipeline_mode=pl.Buffered(1))


def _split3(a):
    hi = a.astype(BF16)
    r = a - hi.astype(F32)
    mid = r.astype(BF16)
    lo = (r - mid.astype(F32)).astype(BF16)
    return hi, mid, lo


def _dot(a, b):
    return jnp.dot(a, b, preferred_element_type=F32)


def _dot_nt(a, b):
    return lax.dot_general(a, b, (((1,), (1,)), ((), ())), preferred_element_type=F32)


def _dot_tn(a, b):
    return lax.dot_general(a, b, (((0,), (0,)), ((), ())), preferred_element_type=F32)


def _dot_f32_lhs(a, b_exact):
    hi, mid, lo = _split3(a)
    return _dot(hi, b_exact) + _dot(mid, b_exact) + _dot(lo, b_exact)


def _dot_f32_rhs(a_exact, b):
    hi, mid, lo = _split3(b)
    return _dot(a_exact, hi) + _dot(a_exact, mid) + _dot(a_exact, lo)


def _softplus(x):
    return jnp.maximum(x, 0.0) + jnp.log1p(jnp.exp(-jnp.abs(x)))


def _silu(x):
    return x * jax.nn.sigmoid(x)


def _layer_norm(t, g, b):
    mu = jnp.mean(t, axis=-1, keepdims=True)
    d = t - mu
    var = jnp.mean(d * d, axis=-1, keepdims=True)
    return d * lax.rsqrt(var + LN_EPS) * g + b


def _inproj_kernel(x_ref, w_ref, wdt_ref, cw_ref, cb_ref, zg_ref, xs_ref, bc_ref, dt_ref, ubuf,
                   *, tn, d_inner):
    @pl.when(pl.program_id(1) == 0)
    def _():
        ubuf[0:8, :] = jnp.zeros((8, ubuf.shape[1]), F32)

    xb = x_ref[...].astype(BF16)
    tm = xb.shape[0]
    dt_ref[...] = _dot(xb, wdt_ref[...])
    n_chunks = w_ref.shape[1] // tn
    proj = lambda c: _dot(xb, w_ref[:, c * tn:(c + 1) * tn])
    nxt = proj(0)
    for c in range(n_chunks):
        cur, nxt = nxt, (proj(c + 1) if c + 1 < n_chunks else None)
        if c * tn < d_inner:
            zg_ref[:, c * tn:(c + 1) * tn] = _silu(cur)
            continue
        o = c * tn - d_inner
        cs = slice(o, o + tn)
        taps = [cw_ref[k:k + 1, cs] for k in range(CONV_K)]
        acc = taps[0] * cur
        for k in range(1, CONV_K):
            acc = pltpu.roll(acc, 1, axis=0) + taps[k] * cur
        ubuf[8:16, cs] = cur[0:8]
        top = taps[CONV_K - 1] * cur[0:8]
        for k in range(CONV_K - 1):
            off = 8 - (CONV_K - 1) + k
            top = top + taps[k] * ubuf[off:off + 8, cs]
        ubuf[0:8, cs] = cur[tm - 8:tm]
        act = _silu(acc + cb_ref[:, cs])
        act_top = _silu(top + cb_ref[:, cs])
        if o < d_inner:
            xs_ref[8:tm, cs] = act[8:tm]
            xs_ref[0:8, cs] = act_top
        else:
            bs = slice(o - d_inner, o - d_inner + tn)
            bc_ref[:, bs] = jnp.concatenate([act_top, act[8:tm]], axis=0).astype(BF16)


def _inproj(x2d, w, wdt, conv_w, conv_b, *, batch, seq, d_inner, tm=512, tn=256):
    t, d = x2d.shape
    n_conv = conv_w.shape[1]
    assert w.shape[1] == d_inner + n_conv and n_conv == 2 * d_inner
    nblk = seq // tm
    row = lambda b, s: (b * nblk + s, 0)
    return pl.pallas_call(
        functools.partial(_inproj_kernel, tn=tn, d_inner=d_inner),
        grid=(batch, nblk),
        in_specs=[pl.BlockSpec((tm, d), row),
                  _const_spec(w.shape), _const_spec(wdt.shape),
                  _const_spec(conv_w.shape), _const_spec(conv_b.shape)],
        out_specs=[pl.BlockSpec((tm, d_inner), row),
                   pl.BlockSpec((tm, d_inner), row),
                   pl.BlockSpec((tm, d_inner), row),
                   pl.BlockSpec((tm, LANES), row)],
        out_shape=[jax.ShapeDtypeStruct((t, d_inner), F32),
                   jax.ShapeDtypeStruct((t, d_inner), F32),
                   jax.ShapeDtypeStruct((t, d_inner), BF16),
                   jax.ShapeDtypeStruct((t, LANES), F32)],
        scratch_shapes=[pltpu.VMEM((16, n_conv), F32)],
        compiler_params=_cparams("parallel", "arbitrary"),
        name="in_proj",
    )(x2d, w, wdt, conv_w, conv_b)


def _ssd_kernel(zg_ref, xact, bc_ref, dtr_ref, dtb_ref, alog_ref, dsk_ref, nw_ref, e_ref, tril_ref,
                y_ref,
                yacc, state_ref, *, rows, d_inner, n_bc):
    @pl.when(pl.program_id(1) == 0)
    def _():
        state_ref[...] = jnp.zeros(state_ref.shape, F32)

    dt = _softplus(dtr_ref[...] + dtb_ref[...])
    a_neg = -jnp.exp(alog_ref[...])
    cum = _dot_f32_rhs(tril_ref[...], dt * a_neg)
    dt3 = _split3(dt)
    cum3 = _split3(cum)
    expand = lambda parts, e: _dot(parts[0], e) + _dot(parts[1], e) + _dot(parts[2], e)

    gw = (d_inner // SSM_GROUPS)
    hpg = gw // SSM_HEAD_DIM
    nch = rows // CHUNK

    li = lax.broadcasted_iota(jnp.int32, (CHUNK, gw), 0)
    si = lax.broadcasted_iota(jnp.int32, (CHUNK, gw), 1) % SSM_HEAD_DIM
    causal = li >= si
    diag = li == si
    bi = lax.broadcasted_iota(jnp.int32, (hpg * CHUNK, gw), 0) // CHUNK
    bj = lax.broadcasted_iota(jnp.int32, (hpg * CHUNK, gw), 1) // SSM_HEAD_DIM
    blockdiag = bi == bj

    for g in range(SSM_GROUPS):
        gs = slice(g * gw, (g + 1) * gw)
        ns = slice(g * SSM_STATE, (g + 1) * SSM_STATE)
        e_g = e_ref[:, gs]
        dtx = expand(dt3, e_g)
        cumx = expand(cum3, e_g)
        xs = xact[:, gs]
        xdt = xs * dtx
        ecum = jnp.exp(cumx)
        for c in range(nch):
            rs = slice(c * CHUNK, (c + 1) * CHUNK)
            cumx_c = cumx[rs]
            cum_end = cumx_c[CHUNK - 1:CHUNK, :]
            bg = bc_ref[rs, ns]
            cg = bc_ref[rs, n_bc + g * SSM_STATE:n_bc + (g + 1) * SSM_STATE]
            xdt_c = xdt[rs]
            st = state_ref[g]
            y_state = _dot(cg, st.astype(BF16)) * ecum[rs]
            cb4 = _dot_nt(cg, jnp.concatenate([bg] * hpg, axis=0))
            cum_row = jnp.sum(jnp.where(diag, cumx_c, 0.0), axis=0, keepdims=True)
            seg = jnp.where(causal, cumx_c - cum_row, -jnp.inf)
            w4 = (jnp.exp(seg) * cb4).astype(BF16)
            x4 = jnp.concatenate([xdt_c.astype(BF16)] * hpg, axis=0)
            x4 = jnp.where(blockdiag, x4, jnp.zeros_like(x4))
            yacc[rs, gs] = _dot(w4, x4) + y_state
            xw = (xdt_c * jnp.exp(cum_end - cumx_c)).astype(BF16)
            state_ref[g] = st * jnp.exp(cum_end) + _dot_tn(bg, xw)

    for g in range(SSM_GROUPS):
        gs = slice(g * gw, (g + 1) * gw)
        y = yacc[:, gs] + dsk_ref[:, gs] * xact[:, gs]
        y = y * zg_ref[:, gs]
        ms = jnp.mean(y * y, axis=-1, keepdims=True)
        y_ref[:, gs] = (y * lax.rsqrt(ms + RMS_EPS) * nw_ref[:, gs]).astype(y_ref.dtype)


def _ssd(zg, xs, bc, dt_raw, dt_bias, a_log, d_skip, norm_w, expand, tril,
         *, batch, seq, d_inner, rows):
    t = zg.shape[0]
    n_bc = bc.shape[1] // 2
    nblk = seq // rows
    row = lambda b, s: (b * nblk + s, 0)
    kern = functools.partial(_ssd_kernel, rows=rows, d_inner=d_inner, n_bc=n_bc)
    return pl.pallas_call(
        kern,
        grid=(batch, nblk),
        in_specs=[pl.BlockSpec((rows, d_inner), row),
                  pl.BlockSpec((rows, d_inner), row),
                  pl.BlockSpec((rows, 2 * n_bc), row),
                  pl.BlockSpec((rows, LANES), row),
                  _const_spec(dt_bias.shape), _const_spec(a_log.shape),
                  _const_spec(d_skip.shape), _const_spec(norm_w.shape),
                  _const_spec(expand.shape), _const_spec(tril.shape)],
        out_specs=pl.BlockSpec((rows, d_inner), row),
        out_shape=jax.ShapeDtypeStruct((t, d_inner), BF16),
        scratch_shapes=[pltpu.VMEM((rows, d_inner), F32),
                        pltpu.VMEM((SSM_GROUPS, SSM_STATE, d_inner // SSM_GROUPS), F32)],
        compiler_params=_cparams("parallel", "arbitrary"),
        name="ssd_scan",
    )(zg, xs, bc, dt_raw, dt_bias, a_log, d_skip, norm_w, expand, tril)


def _proj_ln_kernel(a_ref, w_ref, x_ref, g_ref, b_ref, o_ref, *, alpha):
    h = _dot(a_ref[...], w_ref[...])
    o_ref[...] = _layer_norm(alpha * x_ref[...] + h, g_ref[...], b_ref[...])


def _proj_ln(a, w, x2d, g, b, *, alpha, tm=512):
    t, k = a.shape
    d = w.shape[1]
    return pl.pallas_call(
        functools.partial(_proj_ln_kernel, alpha=alpha),
        grid=(t // tm,),
        in_specs=[pl.BlockSpec((tm, k), lambda i: (i, 0)),
                  _const_spec((k, d)),
                  pl.BlockSpec((tm, d), lambda i: (i, 0)),
                  _const_spec((1, d)), _const_spec((1, d))],
        out_specs=pl.BlockSpec((tm, d), lambda i: (i, 0)),
        out_shape=jax.ShapeDtypeStruct((t, d), F32),
        compiler_params=_cparams("parallel"),
        name="proj_ln",
    )(a, w, x2d, g, b)


def _ffn_kernel(x_ref, wg_ref, wu_ref, wd_ref, g_ref, b_ref, o_ref, a_scr, *, alpha, tf):
    x = x_ref[...]
    xb = x.astype(BF16)
    d_ff = wd_ref.shape[0]
    for c in range(d_ff // tf):
        cs = slice(c * tf, (c + 1) * tf)
        a_scr[:, cs] = (_silu(_dot(xb, wg_ref[:, cs])) * _dot(xb, wu_ref[:, cs])).astype(BF16)
    h = _dot(a_scr[...], wd_ref[...])
    o_ref[...] = _layer_norm(alpha * x + h, g_ref[...], b_ref[...])


def _ffn(x2d, wg, wu, wd, g, b, *, alpha, tm=512, tf=256):
    t, d = x2d.shape
    d_ff = wd.shape[0]
    assert d_ff % tf == 0
    return pl.pallas_call(
        functools.partial(_ffn_kernel, alpha=alpha, tf=tf),
        grid=(t // tm,),
        in_specs=[pl.BlockSpec((tm, d), lambda i: (i, 0)),
                  _const_spec((d, d_ff)), _const_spec((d, d_ff)),
                  _const_spec((d_ff, d)),
                  _const_spec((1, d)), _const_spec((1, d))],
        out_specs=pl.BlockSpec((tm, d), lambda i: (i, 0)),
        out_shape=jax.ShapeDtypeStruct((t, d), F32),
        scratch_shapes=[pltpu.VMEM((tm, d_ff), BF16)],
        compiler_params=_cparams("parallel"),
        name="ffn_ln",
    )(x2d, wg, wu, wd, g, b)


def _qkv_kernel(x_ref, w_ref, wvt_ref, wf_ref, bf_ref, pq_ref, pk_ref, cq_ref, ck_ref, tril_ref,
                q_ref, k_ref, vt_ref, carry, *, att_dim, n_heads, scale):
    @pl.when(pl.program_id(1) == 0)
    def _():
        carry[...] = jnp.zeros(carry.shape, F32)

    xb = x_ref[...].astype(BF16)
    tm = xb.shape[0]
    lane = lax.broadcasted_iota(jnp.int32, (tm, LANES), 1)

    logf = -_softplus(-(_dot(xb, wf_ref[...]) + bf_ref[...]))
    logf = jnp.where(lane < n_heads, logf, 0.0)
    fc = _dot_f32_rhs(tril_ref[...], logf) + carry[...]
    carry[...] = fc[tm - 1:tm, :]
    hi, mid, lo = _split3(fc)
    feats = (hi.astype(F32) + pltpu.roll(mid.astype(F32), n_heads, axis=1)
             + pltpu.roll(lo.astype(F32), 2 * n_heads, axis=1)).astype(BF16)
    q_aug = _dot(feats, pq_ref[...]) + cq_ref[...]
    k_aug = _dot(feats, pk_ref[...]) + ck_ref[...]

    low = lane < ATT_HEAD_DIM
    tn = 2 * LANES
    for c in range(att_dim // tn):
        q4 = _dot(xb, w_ref[:, c * tn:(c + 1) * tn]) * scale
        k4 = _dot(xb, w_ref[:, att_dim + c * tn:att_dim + (c + 1) * tn])
        for h in range(tn // ATT_HEAD_DIM):
            ps = slice((h // 2) * LANES, (h // 2 + 1) * LANES)
            qh, kh = q4[:, ps], k4[:, ps]
            if h % 2:
                qh, kh = (pltpu.roll(a, ATT_HEAD_DIM, axis=1) for a in (qh, kh))
            head = c * (tn // ATT_HEAD_DIM) + h
            hs = slice(head * LANES, (head + 1) * LANES)
            q_ref[:, hs] = jnp.where(low, qh, q_aug[:, hs]).astype(BF16)
            k_ref[:, hs] = jnp.where(low, kh, k_aug[:, hs]).astype(BF16)
    vt_ref[0] = _dot_nt(wvt_ref[...], xb).astype(BF16)


def _qkv(x2d, w, wvt, wf, bf, pq, pk, cq, ck, tril, *, batch, seq, att_dim, n_heads, scale, tm):
    t, d = x2d.shape
    nblk = seq // tm
    wide = n_heads * LANES
    row = lambda b, s: (b * nblk + s, 0)
    kern = functools.partial(_qkv_kernel, att_dim=att_dim, n_heads=n_heads, scale=scale)
    return pl.pallas_call(
        kern,
        grid=(batch, nblk),
        in_specs=[pl.BlockSpec((tm, d), row),
                  _const_spec(w.shape), _const_spec(wvt.shape), _const_spec(wf.shape), _const_spec(bf.shape),
                  _const_spec(pq.shape), _const_spec(pk.shape),
                  _const_spec(cq.shape), _const_spec(ck.shape), _const_spec(tril.shape)],
        out_specs=[pl.BlockSpec((tm, wide), row), pl.BlockSpec((tm, wide), row),
                   pl.BlockSpec((1, att_dim, tm), lambda b, s: (b, 0, s))],
        out_shape=[jax.ShapeDtypeStruct((t, wide), BF16), jax.ShapeDtypeStruct((t, wide), BF16),
                   jax.ShapeDtypeStruct((batch, att_dim, seq), BF16)],
        scratch_shapes=[pltpu.VMEM((1, LANES), F32)],
        compiler_params=_cparams("parallel", "arbitrary"),
        name="qkv_proj",
    )(x2d, w, wvt, wf, bf, pq, pk, cq, ck, tril)


def _attn_kernel(q_ref, k_ref, vt_ref, o_ref, vaug, *, tq, tk, seq):
    d = ATT_HEAD_DIM
    ones = jnp.ones((SUM_ROWS, seq), BF16)
    for hh in range(2):
        vaug[hh, 0:d, :] = vt_ref[0, hh * d:(hh + 1) * d, :]
        vaug[hh, d:d + SUM_ROWS, :] = ones
    row = lax.broadcasted_iota(jnp.int32, (tk, tq), 0)
    col = lax.broadcasted_iota(jnp.int32, (tk, tq), 1)

    def n_key_tiles(qi):
        return -(-(qi + 1) * tq // tk)

    def scores(qi, j, hh):
        hs = slice(hh * LANES, (hh + 1) * LANES)
        s = _dot_nt(k_ref[0, j * tk:(j + 1) * tk, hs], q_ref[0, qi * tq:(qi + 1) * tq, hs])
        if (j + 1) * tk - 1 > qi * tq:
            s = jnp.where(row + j * tk <= col + qi * tq, s, -jnp.inf)
        return s

    items = [(qi, j, hh) for qi in range(seq // tq) for j in range(n_key_tiles(qi)) for hh in range(2)]
    state = {}
    outs = {}
    s_next = scores(*items[0])
    for idx, (qi, j, hh) in enumerate(items):
        s = s_next
        if idx + 1 < len(items):
            s_next = scores(*items[idx + 1])
        cm = jnp.max(s, axis=0, keepdims=True)
        if j == 0:
            m_new = cm
            acc = _dot(vaug[hh, :, j * tk:(j + 1) * tk], jnp.exp(s - m_new).astype(BF16))
        else:
            m, acc = state[hh]
            m_new = jnp.maximum(m, cm)
            pv = _dot(vaug[hh, :, j * tk:(j + 1) * tk], jnp.exp(s - m_new).astype(BF16))
            acc = jnp.exp(m - m_new) * acc + pv
        state[hh] = (m_new, acc)
        if j == n_key_tiles(qi) - 1:
            outs[hh] = acc[0:d] / acc[d:d + 1]
            if hh == 1:
                o_ref[0, qi * tq:(qi + 1) * tq, :] = (
                    jnp.concatenate([outs[0], outs[1]], axis=0).T.astype(o_ref.dtype))


def _attention(q, k, vt, *, batch, seq, n_heads, tq, tk):
    wide = n_heads * LANES
    q3, k3 = (a.reshape(batch, seq, wide) for a in (q, k))
    pair = lambda b, p: (b, 0, p)
    return pl.pallas_call(
        functools.partial(_attn_kernel, tq=tq, tk=tk, seq=seq),
        grid=(batch, n_heads // 2),
        in_specs=[pl.BlockSpec((1, seq, 2 * LANES), pair),
                  pl.BlockSpec((1, seq, 2 * LANES), pair),
                  pl.BlockSpec((1, 2 * ATT_HEAD_DIM, seq), lambda b, p: (b, p, 0))],
        out_specs=pl.BlockSpec((1, seq, LANES), pair),
        out_shape=jax.ShapeDtypeStruct((batch, seq, n_heads * ATT_HEAD_DIM), BF16),
        scratch_shapes=[pltpu.VMEM((2, ATT_HEAD_DIM + SUM_ROWS, seq), BF16)],
        compiler_params=_cparams("parallel", "parallel"),
        name="fox_attention",
    )(q3, k3, vt)


def _pad_lanes(a, n=LANES):
    return jnp.pad(a, [(0, 0)] * (a.ndim - 1) + [(0, n - a.shape[-1])])


def _block_tril(n, block):
    i = jnp.arange(n)
    return ((i[:, None] >= i[None, :]) & (i[:, None] // block == i[None, :] // block)).astype(BF16)


def kernel(x, ssm_in_w, ssm_conv_w, ssm_conv_b, ssm_dt_bias, ssm_a_log, ssm_d, ssm_norm_w, ssm_out_w,
           kv_w, kv_b_f, att_q_w, att_o_w, ffn_gate_w, ffn_up_w, ffn_down_w,
           ln_mix_g, ln_mix_b, ln_ffn_g, ln_ffn_b):
    batch, seq, d_model = x.shape
    depth = ffn_gate_w.shape[0]
    assert ssm_in_w.shape[0] == 1 and att_q_w.shape[0] == 1 and depth == 2
    alpha = (2.0 * depth) ** 0.25
    d_inner = ssm_out_w.shape[1]
    n_conv = ssm_conv_w.shape[2]
    n_ssm_heads = ssm_dt_bias.shape[1]
    att_dim = att_q_w.shape[2]
    n_att_heads = att_dim // ATT_HEAD_DIM
    tf = 256
    ssd_rows = 256
    t = batch * seq
    x2d = x.reshape(t, d_model)
    row2 = lambda a: a.reshape(1, -1).astype(F32)

    in_w = ssm_in_w[0]
    n_main = d_inner + n_conv
    zg, xs, bc, dt_raw = _inproj(x2d, in_w[:, :n_main].astype(BF16), _pad_lanes(in_w[:, n_main:]).astype(BF16),
                                 ssm_conv_w[0], row2(ssm_conv_b[0]), batch=batch, seq=seq, d_inner=d_inner)
    head_of_lane = jnp.arange(d_inner) // SSM_HEAD_DIM
    expand = (jnp.arange(LANES)[:, None] == head_of_lane[None, :]).astype(BF16)
    y = _ssd(zg, xs, bc, dt_raw,
             _pad_lanes(row2(ssm_dt_bias[0])), _pad_lanes(row2(ssm_a_log[0])),
             row2(jnp.repeat(ssm_d[0], SSM_HEAD_DIM)), row2(ssm_norm_w[0]),
             expand, _block_tril(ssd_rows, CHUNK),
             batch=batch, seq=seq, d_inner=d_inner, rows=ssd_rows)
    x2d = _proj_ln(y, ssm_out_w[0].astype(BF16), x2d, row2(ln_mix_g[0]), row2(ln_mix_b[0]), alpha=alpha)
    x2d = _ffn(x2d, ffn_gate_w[0].astype(BF16), ffn_up_w[0].astype(BF16),
               ffn_down_w[0].astype(BF16), row2(ln_ffn_g[0]), row2(ln_ffn_b[0]), alpha=alpha, tf=tf)

    wqk = jnp.concatenate([att_q_w[0], kv_w[:, :att_dim]], axis=1).astype(BF16)
    wvt = kv_w[:, att_dim:2 * att_dim].T.astype(BF16)
    wf = _pad_lanes(kv_w[:, 2 * att_dim:]).astype(BF16)
    wide = n_att_heads * LANES
    feat = jnp.arange(LANES)[:, None]
    slot = jnp.arange(wide)[None, :]
    place = lambda base, sign: sum(
        sign * ((feat == part * n_att_heads + h) & (slot == h * LANES + base + part)).astype(F32)
        for part in range(3) for h in range(n_att_heads))
    in_slot = lambda base: sum((slot == h * LANES + base + part).astype(F32)
                               for part in range(3) for h in range(n_att_heads))
    pq, pk = place(AUG0, 1.0).astype(BF16), place(AUG1, -1.0).astype(BF16)
    cq, ck = in_slot(AUG1), in_slot(AUG0)
    tm_qkv = 512
    q, k, vt = _qkv(x2d, wqk, wvt, wf, _pad_lanes(row2(kv_b_f)), pq, pk, cq, ck,
                    _block_tril(tm_qkv, tm_qkv),
                    batch=batch, seq=seq, att_dim=att_dim, n_heads=n_att_heads,
                    scale=ATT_HEAD_DIM ** -0.5, tm=tm_qkv)

    o = _attention(q, k, vt, batch=batch, seq=seq, n_heads=n_att_heads, tq=512, tk=512)
    x2d = _proj_ln(o.reshape(t, att_dim), att_o_w[0].astype(BF16), x2d,
                   row2(ln_mix_g[1]), row2(ln_mix_b[1]), alpha=alpha)
    x2d = _ffn(x2d, ffn_gate_w[1].astype(BF16), ffn_up_w[1].astype(BF16),
               ffn_down_w[1].astype(BF16), row2(ln_ffn_g[1]), row2(ln_ffn_b[1]), alpha=alpha, tf=tf)
    return x2d.reshape(batch, seq, d_model)
```

```python
import functools

import jax
import jax.numpy as jnp
from jax import lax
from jax.experimental import pallas as pl
from jax.experimental.pallas import tpu as pltpu

F32 = jnp.float32
BF16 = jnp.bfloat16

LANES = 128
VMEM_LIMIT = 56 * 1024 * 1024

SSM_HEAD_DIM = 64
SSM_GROUPS = 8
SSM_STATE = 128
CONV_K = 4
CHUNK = 64
ATT_HEAD_DIM = 64
LN_EPS = 1e-5
RMS_EPS = 1e-5

AUG = ATT_HEAD_DIM
SUM_ROWS = 16


def _cparams(*sem):
    return pltpu.CompilerParams(dimension_semantics=sem, vmem_limit_bytes=VMEM_LIMIT)


def _const_spec(shape):
    nd = len(shape)
    return pl.BlockSpec(shape, lambda *_: (0,) * nd, pipeline_mode=pl.Buffered(1))


def _split3(a):
    hi = a.astype(BF16)
    r = a - hi.astype(F32)
    mid = r.astype(BF16)
    lo = (r - mid.astype(F32)).astype(BF16)
    return hi, mid, lo


def _dot(a, b):
    return jnp.dot(a, b, preferred_element_type=F32)


def _dot_nt(a, b):
    return lax.dot_general(a, b, (((1,), (1,)), ((), ())), preferred_element_type=F32)


def _dot_tn(a, b):
    return lax.dot_general(a, b, (((0,), (0,)), ((), ())), preferred_element_type=F32)


def _dot_f32_rhs(a_exact, b):
    hi, mid, lo = _split3(b)
    return _dot(a_exact, hi) + _dot(a_exact, mid) + _dot(a_exact, lo)


def _softplus(x):
    return jnp.maximum(x, 0.0) + jnp.log1p(jnp.exp(-jnp.abs(x)))


def _silu(x):
    return x * jax.nn.sigmoid(x)


def _layer_norm(t, g, b):
    mu = jnp.mean(t, axis=-1, keepdims=True)
    d = t - mu
    var = jnp.mean(d * d, axis=-1, keepdims=True)
    return d * lax.rsqrt(var + LN_EPS) * g + b


def _inproj_kernel(x_ref, w_ref, wdt_ref, zx_ref, dt_ref, *, tn):
    xb = x_ref[...].astype(BF16)
    n = w_ref.shape[1]
    for c in range(n // tn):
        zx_ref[:, c * tn:(c + 1) * tn] = _dot(xb, w_ref[:, c * tn:(c + 1) * tn])
    dt_ref[...] = _dot(xb, wdt_ref[...])


def _inproj(x2d, w, wdt, *, tm=512, tn=512):
    t, d = x2d.shape
    n = w.shape[1]
    return pl.pallas_call(
        functools.partial(_inproj_kernel, tn=tn),
        grid=(t // tm,),
        in_specs=[pl.BlockSpec((tm, d), lambda i: (i, 0)),
                  _const_spec((d, n)),
                  _const_spec((d, LANES))],
        out_specs=[pl.BlockSpec((tm, n), lambda i: (i, 0)),
                   pl.BlockSpec((tm, LANES), lambda i: (i, 0))],
        out_shape=[jax.ShapeDtypeStruct((t, n), F32),
                   jax.ShapeDtypeStruct((t, LANES), F32)],
        compiler_params=_cparams("parallel"),
        name="in_proj",
    )(x2d, w, wdt)


def _ssd_kernel(z_ref, xs_ref, bc_ref, dtr_ref, cw_ref, cb_ref, dtb_ref, alog_ref,
                dsk_ref, nw_ref, e_ref, tril_ref,
                y_ref,
                ubuf, xact, bact, cact, yacc, cumx_s, xdt_s, state_ref, *, rows, d_inner, n_bc):
    @pl.when(pl.program_id(1) == 0)
    def _():
        ubuf[0:8, :] = jnp.zeros((8, ubuf.shape[1]), F32)
        state_ref[...] = jnp.zeros(state_ref.shape, F32)

    ubuf[8:8 + rows, 0:d_inner] = xs_ref[...]
    ubuf[8:8 + rows, d_inner:] = bc_ref[...]

    cw = 512
    n_conv = ubuf.shape[1]
    for c in range(n_conv // cw):
        cs = slice(c * cw, (c + 1) * cw)
        acc = cb_ref[:, cs] + cw_ref[CONV_K - 1:CONV_K, cs] * ubuf[8:8 + rows, cs]
        for k in range(CONV_K - 1):
            off = 8 - (CONV_K - 1) + k
            acc = acc + cw_ref[k:k + 1, cs] * ubuf[off:off + rows, cs]
        act = _silu(acc)
        if c * cw < d_inner:
            xact[:, cs] = act
        elif c * cw < d_inner + n_bc:
            bact[:, c * cw - d_inner:(c + 1) * cw - d_inner] = act.astype(BF16)
        else:
            o = c * cw - d_inner - n_bc
            cact[:, o:o + cw] = act.astype(BF16)
    ubuf[0:8, :] = ubuf[rows:rows + 8, :]

    dt = _softplus(dtr_ref[...] + dtb_ref[...])
    a_neg = -jnp.exp(alog_ref[...])
    cum = _dot_f32_rhs(tril_ref[...], dt * a_neg)
    dt3 = _split3(dt)
    cum3 = _split3(cum)
    expand = lambda parts, e: _dot(parts[0], e) + _dot(parts[1], e) + _dot(parts[2], e)

    gw = (d_inner // SSM_GROUPS)
    hpg = gw // SSM_HEAD_DIM
    nch = rows // CHUNK

    li = lax.broadcasted_iota(jnp.int32, (CHUNK, gw), 0)
    si = lax.broadcasted_iota(jnp.int32, (CHUNK, gw), 1) % SSM_HEAD_DIM
    causal = li >= si
    diag = li == si
    bi = lax.broadcasted_iota(jnp.int32, (hpg * CHUNK, gw), 0) // CHUNK
    bj = lax.broadcasted_iota(jnp.int32, (hpg * CHUNK, gw), 1) // SSM_HEAD_DIM
    blockdiag = bi == bj

    for g in range(SSM_GROUPS):
        gs = slice(g * gw, (g + 1) * gw)
        e_g = e_ref[:, gs]
        cumx_s[:, gs] = expand(cum3, e_g)
        xdt_s[:, gs] = xact[:, gs] * expand(dt3, e_g)

    def cb_tile(c, g):
        rs = slice(c * CHUNK, (c + 1) * CHUNK)
        ns = slice(g * SSM_STATE, (g + 1) * SSM_STATE)
        return _dot_nt(cact[rs, ns], jnp.concatenate([bact[rs, ns]] * hpg, axis=0))

    items = [(c, g) for c in range(nch) for g in range(SSM_GROUPS)]
    cb_next = cb_tile(*items[0])
    for idx, (c, g) in enumerate(items):
        cb4 = cb_next
        if idx + 1 < len(items):
            cb_next = cb_tile(*items[idx + 1])
        rs = slice(c * CHUNK, (c + 1) * CHUNK)
        gs = slice(g * gw, (g + 1) * gw)
        ns = slice(g * SSM_STATE, (g + 1) * SSM_STATE)
        cumx_c = cumx_s[rs, gs]
        cum_end = cumx_c[CHUNK - 1:CHUNK, :]
        xdt_c = xdt_s[rs, gs]
        st = state_ref[g]
        y_state = _dot(cact[rs, ns], st.astype(BF16)) * jnp.exp(cumx_c)
        cum_row = jnp.sum(jnp.where(diag, cumx_c, 0.0), axis=0, keepdims=True)
        seg = jnp.where(causal, cumx_c - cum_row, -jnp.inf)
        w4 = (jnp.exp(seg) * cb4).astype(BF16)
        x4 = jnp.concatenate([xdt_c.astype(BF16)] * hpg, axis=0)
        x4 = jnp.where(blockdiag, x4, jnp.zeros_like(x4))
        yacc[rs, gs] = _dot(w4, x4) + y_state
        xw = (xdt_c * jnp.exp(cum_end - cumx_c)).astype(BF16)
        state_ref[g] = st * jnp.exp(cum_end) + _dot_tn(bact[rs, ns], xw)

    for g in range(SSM_GROUPS):
        gs = slice(g * gw, (g + 1) * gw)
        y = yacc[:, gs] + dsk_ref[:, gs] * xact[:, gs]
        y = y * _silu(z_ref[:, gs])
        ms = jnp.mean(y * y, axis=-1, keepdims=True)
        y_ref[:, gs] = (y * lax.rsqrt(ms + RMS_EPS) * nw_ref[:, gs]).astype(y_ref.dtype)


def _ssd(zx, dt_raw, conv_w, conv_b, dt_bias, a_log, d_skip, norm_w, expand, tril,
         *, batch, seq, d_inner, rows):
    t = zx.shape[0]
    n_conv = conv_w.shape[1]
    n_bc = (n_conv - d_inner) // 2
    nblk = seq // rows
    assert zx.shape[1] == d_inner + n_conv and n_conv == 2 * d_inner
    row_map = lambda col: (lambda b, s: (b * nblk + s, col))
    kern = functools.partial(_ssd_kernel, rows=rows, d_inner=d_inner, n_bc=n_bc)
    return pl.pallas_call(
        kern,
        grid=(batch, nblk),
        in_specs=[pl.BlockSpec((rows, d_inner), row_map(0)),
                  pl.BlockSpec((rows, d_inner), row_map(1)),
                  pl.BlockSpec((rows, d_inner), row_map(2)),
                  pl.BlockSpec((rows, LANES), row_map(0)),
                  _const_spec(conv_w.shape), _const_spec(conv_b.shape),
                  _const_spec(dt_bias.shape), _const_spec(a_log.shape),
                  _const_spec(d_skip.shape), _const_spec(norm_w.shape),
                  _const_spec(expand.shape), _const_spec(tril.shape)],
        out_specs=pl.BlockSpec((rows, d_inner), row_map(0)),
        out_shape=jax.ShapeDtypeStruct((t, d_inner), BF16),
        scratch_shapes=[pltpu.VMEM((rows + 8, n_conv), F32),
                        pltpu.VMEM((rows, d_inner), F32),
                        pltpu.VMEM((rows, n_bc), BF16),
                        pltpu.VMEM((rows, n_bc), BF16),
                        pltpu.VMEM((rows, d_inner), F32),
                        pltpu.VMEM((rows, d_inner), F32),
                        pltpu.VMEM((rows, d_inner), F32),
                        pltpu.VMEM((SSM_GROUPS, SSM_STATE, d_inner // SSM_GROUPS), F32)],
        compiler_params=_cparams("parallel", "arbitrary"),
        name="ssd_scan",
    )(zx, zx, zx, dt_raw, conv_w, conv_b, dt_bias, a_log, d_skip, norm_w, expand, tril)


def _proj_ln_kernel(a_ref, w_ref, x_ref, g_ref, b_ref, o_ref, *, alpha, parts):
    rows = a_ref.shape[0] // parts
    for r in range(parts):
        rs = slice(r * rows, (r + 1) * rows)
        h = _dot(a_ref[rs, :], w_ref[...])
        o_ref[rs, :] = _layer_norm(alpha * x_ref[rs, :] + h, g_ref[...], b_ref[...])


def _proj_ln(a, w, x2d, g, b, *, alpha, tm=512, parts=4):
    t, k = a.shape
    d = w.shape[1]
    return pl.pallas_call(
        functools.partial(_proj_ln_kernel, alpha=alpha, parts=parts),
        grid=(t // tm,),
        in_specs=[pl.BlockSpec((tm, k), lambda i: (i, 0)),
                  _const_spec((k, d)),
                  pl.BlockSpec((tm, d), lambda i: (i, 0)),
                  _const_spec((1, d)), _const_spec((1, d))],
        out_specs=pl.BlockSpec((tm, d), lambda i: (i, 0)),
        out_shape=jax.ShapeDtypeStruct((t, d), F32),
        compiler_params=_cparams("parallel"),
        name="proj_ln",
    )(a, w, x2d, g, b)


def _ffn_kernel(x_ref, wg_ref, wu_ref, wd_ref, g_ref, b_ref, o_ref, a_scr, *, alpha, tf):
    x = x_ref[...]
    xb = x.astype(BF16)
    d_ff = wd_ref.shape[0]
    for c in range(d_ff // tf):
        cs = slice(c * tf, (c + 1) * tf)
        a_scr[:, cs] = (_silu(_dot(xb, wg_ref[:, cs])) * _dot(xb, wu_ref[:, cs])).astype(BF16)
    h = _dot(a_scr[...], wd_ref[...])
    o_ref[...] = _layer_norm(alpha * x + h, g_ref[...], b_ref[...])


def _ffn(x2d, wg, wu, wd, g, b, *, alpha, tm=512, tf=256):
    t, d = x2d.shape
    d_ff = wd.shape[0]
    assert d_ff % tf == 0
    return pl.pallas_call(
        functools.partial(_ffn_kernel, alpha=alpha, tf=tf),
        grid=(t // tm,),
        in_specs=[pl.BlockSpec((tm, d), lambda i: (i, 0)),
                  _const_spec((d, d_ff)), _const_spec((d, d_ff)),
                  _const_spec((d_ff, d)),
                  _const_spec((1, d)), _const_spec((1, d))],
        out_specs=pl.BlockSpec((tm, d), lambda i: (i, 0)),
        out_shape=jax.ShapeDtypeStruct((t, d), F32),
        scratch_shapes=[pltpu.VMEM((tm, d_ff), BF16)],
        compiler_params=_cparams("parallel"),
        name="ffn_ln",
    )(x2d, wg, wu, wd, g, b)


def _qkv_kernel(x_ref, w_ref, wvt_ref, wf_ref, bf_ref, pk_ref, cq_ref, tril_ref,
                q_ref, k_ref, vt_ref, carry, *, att_dim, n_heads, scale):
    @pl.when(pl.program_id(1) == 0)
    def _():
        carry[...] = jnp.zeros(carry.shape, F32)

    xb = x_ref[...].astype(BF16)
    tm = xb.shape[0]
    lane = lax.broadcasted_iota(jnp.int32, (tm, LANES), 1)

    logf = -_softplus(-(_dot(xb, wf_ref[...]) + bf_ref[...]))
    logf = jnp.where(lane < n_heads, logf, 0.0)
    fc = _dot_f32_rhs(tril_ref[...], logf) + carry[...]
    carry[...] = fc[tm - 1:tm, :]
    hi, mid, lo = _split3(fc)
    feats = (hi.astype(F32) + pltpu.roll(mid.astype(F32), n_heads, axis=1)
             + pltpu.roll(lo.astype(F32), 2 * n_heads, axis=1)).astype(BF16)
    k_aug = _dot(feats, pk_ref[...])

    low = lane < ATT_HEAD_DIM
    tn = 2 * LANES
    for c in range(att_dim // tn):
        q4 = _dot(xb, w_ref[:, c * tn:(c + 1) * tn]) * scale
        k4 = _dot(xb, w_ref[:, att_dim + c * tn:att_dim + (c + 1) * tn])
        for h in range(tn // ATT_HEAD_DIM):
            ps = slice((h // 2) * LANES, (h // 2 + 1) * LANES)
            qh, kh = q4[:, ps], k4[:, ps]
            if h % 2:
                qh, kh = (pltpu.roll(a, ATT_HEAD_DIM, axis=1) for a in (qh, kh))
            head = c * (tn // ATT_HEAD_DIM) + h
            hs = slice(head * LANES, (head + 1) * LANES)
            q_ref[:, hs] = jnp.where(low, qh, cq_ref[...]).astype(BF16)
            k_ref[:, hs] = jnp.where(low, kh, k_aug[:, hs]).astype(BF16)
    vt_ref[0] = _dot_nt(wvt_ref[...], xb).astype(BF16)


def _qkv(x2d, w, wvt, wf, bf, pk, cq, tril, *, batch, seq, att_dim, n_heads, scale, tm):
    t, d = x2d.shape
    nblk = seq // tm
    wide = n_heads * LANES
    row = lambda b, s: (b * nblk + s, 0)
    kern = functools.partial(_qkv_kernel, att_dim=att_dim, n_heads=n_heads, scale=scale)
    return pl.pallas_call(
        kern,
        grid=(batch, nblk),
        in_specs=[pl.BlockSpec((tm, d), row),
                  _const_spec(w.shape), _const_spec(wvt.shape), _const_spec(wf.shape), _const_spec(bf.shape),
                  _const_spec(pk.shape), _const_spec(cq.shape), _const_spec(tril.shape)],
        out_specs=[pl.BlockSpec((tm, wide), row), pl.BlockSpec((tm, wide), row),
                   pl.BlockSpec((1, att_dim, tm), lambda b, s: (b, 0, s))],
        out_shape=[jax.ShapeDtypeStruct((t, wide), BF16), jax.ShapeDtypeStruct((t, wide), BF16),
                   jax.ShapeDtypeStruct((batch, att_dim, seq), BF16)],
        scratch_shapes=[pltpu.VMEM((1, LANES), F32)],
        compiler_params=_cparams("parallel", "arbitrary"),
        name="qkv_proj",
    )(x2d, w, wvt, wf, bf, pk, cq, tril)


def _attn_kernel(q_ref, k_ref, vt_ref, o_ref, vaug, *, tq, tk, seq):
    d = ATT_HEAD_DIM
    ones = jnp.ones((SUM_ROWS, seq), BF16)
    for hh in range(2):
        vaug[hh, 0:d, :] = vt_ref[0, hh * d:(hh + 1) * d, :]
        vaug[hh, d:d + SUM_ROWS, :] = ones
    row = lax.broadcasted_iota(jnp.int32, (tk, tq), 0)
    col = lax.broadcasted_iota(jnp.int32, (tk, tq), 1)

    def n_key_tiles(qi):
        return -(-(qi + 1) * tq // tk)

    def scores(qi, j, hh):
        hs = slice(hh * LANES, (hh + 1) * LANES)
        s = _dot_nt(k_ref[0, j * tk:(j + 1) * tk, hs], q_ref[0, qi * tq:(qi + 1) * tq, hs])
        if (j + 1) * tk - 1 > qi * tq:
            s = jnp.where(row + j * tk <= col + qi * tq, s, -jnp.inf)
        return s

    items = [(qi, j, hh) for qi in range(seq // tq) for j in range(n_key_tiles(qi)) for hh in range(2)]
    m_run, acc_run, outs = {}, {}, {}

    def softmax_part(s, j, hh):
        cm = jnp.max(s, axis=0, keepdims=True)
        m_new = cm if j == 0 else jnp.maximum(m_run[hh], cm)
        rescale = None if j == 0 else jnp.exp(m_run[hh] - m_new)
        m_run[hh] = m_new
        return jnp.exp(s - m_new).astype(BF16), rescale

    def value_part(p, rescale, qi, j, hh):
        pv = _dot(vaug[hh, :, j * tk:(j + 1) * tk], p)
        acc = pv if j == 0 else rescale * acc_run[hh] + pv
        acc_run[hh] = acc
        if j == n_key_tiles(qi) - 1:
            outs[hh] = acc[0:d] / acc[d:d + 1]
            if hh == 1:
                o_ref[0, qi * tq:(qi + 1) * tq, :] = (
                    jnp.concatenate([outs[0], outs[1]], axis=0).T.astype(o_ref.dtype))

    n = len(items)
    s_q = {0: scores(*items[0])}
    if n > 1:
        s_q[1] = scores(*items[1])
    p_q = {0: softmax_part(s_q.pop(0), *items[0][1:])}
    for idx in range(n):
        if idx + 2 < n:
            s_q[idx + 2] = scores(*items[idx + 2])
        if idx + 1 < n:
            p_q[idx + 1] = softmax_part(s_q.pop(idx + 1), *items[idx + 1][1:])
        value_part(*p_q.pop(idx), *items[idx])


def _attention(q, k, vt, *, batch, seq, n_heads, tq, tk):
    wide = n_heads * LANES
    q3, k3 = (a.reshape(batch, seq, wide) for a in (q, k))
    pair = lambda b, p: (b, 0, p)
    return pl.pallas_call(
        functools.partial(_attn_kernel, tq=tq, tk=tk, seq=seq),
        grid=(batch, n_heads // 2),
        in_specs=[pl.BlockSpec((1, seq, 2 * LANES), pair),
                  pl.BlockSpec((1, seq, 2 * LANES), pair),
                  pl.BlockSpec((1, 2 * ATT_HEAD_DIM, seq), lambda b, p: (b, p, 0))],
        out_specs=pl.BlockSpec((1, seq, LANES), pair),
        out_shape=jax.ShapeDtypeStruct((batch, seq, n_heads * ATT_HEAD_DIM), BF16),
        scratch_shapes=[pltpu.VMEM((2, ATT_HEAD_DIM + SUM_ROWS, seq), BF16)],
        compiler_params=_cparams("parallel", "parallel"),
        name="fox_attention",
    )(q3, k3, vt)


def _pad_lanes(a, n=LANES):
    return jnp.pad(a, [(0, 0)] * (a.ndim - 1) + [(0, n - a.shape[-1])])


def _block_tril(n, block):
    i = jnp.arange(n)
    return ((i[:, None] >= i[None, :]) & (i[:, None] // block == i[None, :] // block)).astype(BF16)


def kernel(x, ssm_in_w, ssm_conv_w, ssm_conv_b, ssm_dt_bias, ssm_a_log, ssm_d, ssm_norm_w, ssm_out_w,
           kv_w, kv_b_f, att_q_w, att_o_w, ffn_gate_w, ffn_up_w, ffn_down_w,
           ln_mix_g, ln_mix_b, ln_ffn_g, ln_ffn_b):
    batch, seq, d_model = x.shape
    depth = ffn_gate_w.shape[0]
    assert ssm_in_w.shape[0] == 1 and att_q_w.shape[0] == 1 and depth == 2
    alpha = (2.0 * depth) ** 0.25
    d_inner = ssm_out_w.shape[1]
    n_conv = ssm_conv_w.shape[2]
    att_dim = att_q_w.shape[2]
    n_att_heads = att_dim // ATT_HEAD_DIM
    tf = 256
    ssd_rows = 256
    t = batch * seq
    x2d = x.reshape(t, d_model)
    row2 = lambda a: a.reshape(1, -1).astype(F32)

    in_w = ssm_in_w[0]
    n_main = d_inner + n_conv
    zx, dt_raw = _inproj(x2d, in_w[:, :n_main].astype(BF16), _pad_lanes(in_w[:, n_main:]).astype(BF16))
    head_of_lane = jnp.arange(d_inner) // SSM_HEAD_DIM
    expand = (jnp.arange(LANES)[:, None] == head_of_lane[None, :]).astype(BF16)
    y = _ssd(zx, dt_raw, ssm_conv_w[0], row2(ssm_conv_b[0]),
             _pad_lanes(row2(ssm_dt_bias[0])), _pad_lanes(row2(ssm_a_log[0])),
             row2(jnp.repeat(ssm_d[0], SSM_HEAD_DIM)), row2(ssm_norm_w[0]),
             expand, _block_tril(ssd_rows, CHUNK),
             batch=batch, seq=seq, d_inner=d_inner, rows=ssd_rows)
    x2d = _proj_ln(y, ssm_out_w[0].astype(BF16), x2d, row2(ln_mix_g[0]), row2(ln_mix_b[0]), alpha=alpha)
    x2d = _ffn(x2d, ffn_gate_w[0].astype(BF16), ffn_up_w[0].astype(BF16),
               ffn_down_w[0].astype(BF16), row2(ln_ffn_g[0]), row2(ln_ffn_b[0]), alpha=alpha, tf=tf)

    wqk = jnp.concatenate([att_q_w[0], kv_w[:, :att_dim]], axis=1).astype(BF16)
    wvt = kv_w[:, att_dim:2 * att_dim].T.astype(BF16)
    wf = _pad_lanes(kv_w[:, 2 * att_dim:]).astype(BF16)
    wide = n_att_heads * LANES
    feat = jnp.arange(LANES)[:, None]
    slot = jnp.arange(wide)[None, :]
    pk = sum(-((feat == part * n_att_heads + h) & (slot == h * LANES + AUG + part)).astype(F32)
             for part in range(3) for h in range(n_att_heads)).astype(BF16)
    lane = jnp.arange(LANES)[None, :]
    cq = ((lane >= AUG) & (lane < AUG + 3)).astype(F32)
    tm_qkv = 512
    q, k, vt = _qkv(x2d, wqk, wvt, wf, _pad_lanes(row2(kv_b_f)), pk, cq,
                    _block_tril(tm_qkv, tm_qkv),
                    batch=batch, seq=seq, att_dim=att_dim, n_heads=n_att_heads,
                    scale=ATT_HEAD_DIM ** -0.5, tm=tm_qkv)

    o = _attention(q, k, vt, batch=batch, seq=seq, n_heads=n_att_heads, tq=512, tk=512)
    x2d = _proj_ln(o.reshape(t, att_dim), att_o_w[0].astype(BF16), x2d,
                   row2(ln_mix_g[1]), row2(ln_mix_b[1]), alpha=alpha)
    x2d = _ffn(x2d, ffn_gate_w[1].astype(BF16), ffn_up_w[1].astype(BF16),
               ffn_down_w[1].astype(BF16), row2(ln_ffn_g[1]), row2(ln_ffn_b[1]), alpha=alpha, tf=tf)
    return x2d.reshape(batch, seq, d_model)
```

```python
import functools

import jax
import jax.numpy as jnp
from jax import lax
from jax.experimental import pallas as pl
from jax.experimental.pallas import tpu as pltpu

F32 = jnp.float32
BF16 = jnp.bfloat16

LANES = 128
VMEM_LIMIT = 56 * 1024 * 1024

SSM_HEAD_DIM = 64
SSM_GROUPS = 8
SSM_STATE = 128
CONV_K = 4
CHUNK = 64
ATT_HEAD_DIM = 64
LN_EPS = 1e-5
RMS_EPS = 1e-5

AUG = ATT_HEAD_DIM
SUM_ROWS = 16
QB = 2 * LANES


def _cparams(*sem):
    return pltpu.CompilerParams(dimension_semantics=sem, vmem_limit_bytes=VMEM_LIMIT)


def _const_spec(shape):
    nd = len(shape)
    return pl.BlockSpec(shape, lambda *_: (0,) * nd, pipeline_mode=pl.Buffered(1))


def _split3(a):
    hi = a.astype(BF16)
    r = a - hi.astype(F32)
    mid = r.astype(BF16)
    lo = (r - mid.astype(F32)).astype(BF16)
    return hi, mid, lo


def _dot(a, b):
    return jnp.dot(a, b, preferred_element_type=F32)


def _dot_nt(a, b):
    return lax.dot_general(a, b, (((1,), (1,)), ((), ())), preferred_element_type=F32)


def _dot_tn(a, b):
    return lax.dot_general(a, b, (((0,), (0,)), ((), ())), preferred_element_type=F32)


def _dot_f32_rhs(a_exact, b):
    hi, mid, lo = _split3(b)
    return _dot(a_exact, hi) + _dot(a_exact, mid) + _dot(a_exact, lo)


def _softplus(x):
    return jnp.maximum(x, 0.0) + jnp.log1p(jnp.exp(-jnp.abs(x)))


def _silu(x):
    return x * jax.nn.sigmoid(x)


def _layer_norm(t, g, b):
    mu = jnp.mean(t, axis=-1, keepdims=True)
    d = t - mu
    var = jnp.mean(d * d, axis=-1, keepdims=True)
    return d * lax.rsqrt(var + LN_EPS) * g + b


def _inproj_kernel(x_ref, w_ref, wdt_ref, zx_ref, dt_ref, *, tn, n_gate):
    xb = x_ref[...].astype(BF16)
    n = w_ref.shape[1]
    for c in range(n // tn):
        r = _dot(xb, w_ref[:, c * tn:(c + 1) * tn])
        zx_ref[:, c * tn:(c + 1) * tn] = _silu(r) if c * tn < n_gate else r
    dt_ref[...] = _dot(xb, wdt_ref[...])


def _inproj(x2d, w, wdt, *, n_gate, tm=512, tn=512):
    t, d = x2d.shape
    n = w.shape[1]
    assert n_gate % tn == 0
    return pl.pallas_call(
        functools.partial(_inproj_kernel, tn=tn, n_gate=n_gate),
        grid=(t // tm,),
        in_specs=[pl.BlockSpec((tm, d), lambda i: (i, 0)),
                  _const_spec((d, n)),
                  _const_spec((d, LANES))],
        out_specs=[pl.BlockSpec((tm, n), lambda i: (i, 0)),
                   pl.BlockSpec((tm, LANES), lambda i: (i, 0))],
        out_shape=[jax.ShapeDtypeStruct((t, n), F32),
                   jax.ShapeDtypeStruct((t, LANES), F32)],
        compiler_params=_cparams("parallel"),
        name="in_proj",
    )(x2d, w, wdt)


def _ssd_kernel(zg_ref, xs_ref, bc_ref, dtr_ref, cw_ref, cb_ref, dtb_ref, alog_ref,
                dsk_ref, nw_ref, e_ref, tril_ref,
                y_ref,
                ubuf, stage, xact, bact, cact, yacc, cumx_s, xdt_s, state_ref, *, rows, d_inner, n_bc):
    @pl.when(pl.program_id(1) == 0)
    def _():
        ubuf[:, 0:8, :] = jnp.zeros((ubuf.shape[0], 8, LANES), F32)
        state_ref[...] = jnp.zeros(state_ref.shape, F32)

    half = rows // 2
    for sl in range(ubuf.shape[0]):
        cs = slice(sl * LANES, (sl + 1) * LANES)
        src = xs_ref if sl * LANES < d_inner else bc_ref
        ubuf[sl, 8:8 + rows, :] = src[:, (sl * LANES) % d_inner:(sl * LANES) % d_inner + LANES]
        win = {c: ubuf[sl, pl.ds(8 + c, half, stride=2), :] for c in range(1 - CONV_K, 2)}
        even = cb_ref[:, cs]
        odd = cb_ref[:, cs]
        for k in range(CONV_K):
            tap = cw_ref[k:k + 1, cs]
            even = even + tap * win[k - (CONV_K - 1)]
            odd = odd + tap * win[k - (CONV_K - 1) + 1]
        stage[sl, pl.ds(0, half, stride=2), :] = _silu(even)
        stage[sl, pl.ds(1, half, stride=2), :] = _silu(odd)
        ubuf[sl, 0:8, :] = ubuf[sl, rows:rows + 8, :]
        act = stage[sl]
        if sl * LANES < d_inner:
            xact[:, cs] = act
        elif sl * LANES < d_inner + n_bc:
            bact[:, sl * LANES - d_inner:(sl + 1) * LANES - d_inner] = act.astype(BF16)
        else:
            o = sl * LANES - d_inner - n_bc
            cact[:, o:o + LANES] = act.astype(BF16)

    dt = _softplus(dtr_ref[...] + dtb_ref[...])
    a_neg = -jnp.exp(alog_ref[...])
    cum = _dot_f32_rhs(tril_ref[...], dt * a_neg)
    dt3 = _split3(dt)
    cum3 = _split3(cum)
    expand = lambda parts, e: _dot(parts[0], e) + _dot(parts[1], e) + _dot(parts[2], e)

    gw = (d_inner // SSM_GROUPS)
    hpg = gw // SSM_HEAD_DIM
    nch = rows // CHUNK

    li = lax.broadcasted_iota(jnp.int32, (CHUNK, gw), 0)
    si = lax.broadcasted_iota(jnp.int32, (CHUNK, gw), 1) % SSM_HEAD_DIM
    causal = li >= si
    diag = li == si
    bi = lax.broadcasted_iota(jnp.int32, (hpg * CHUNK, gw), 0) // CHUNK
    bj = lax.broadcasted_iota(jnp.int32, (hpg * CHUNK, gw), 1) // SSM_HEAD_DIM
    blockdiag = bi == bj

    for g in range(SSM_GROUPS):
        gs = slice(g * gw, (g + 1) * gw)
        e_g = e_ref[:, gs]
        cumx_s[:, gs] = expand(cum3, e_g)
        xdt_s[:, gs] = xact[:, gs] * expand(dt3, e_g)

    def cb_tile(c, g):
        rs = slice(c * CHUNK, (c + 1) * CHUNK)
        ns = slice(g * SSM_STATE, (g + 1) * SSM_STATE)
        return _dot_nt(cact[rs, ns], jnp.concatenate([bact[rs, ns]] * hpg, axis=0))

    items = [(c, g) for c in range(nch) for g in range(SSM_GROUPS)]
    cb_next = cb_tile(*items[0])
    for idx, (c, g) in enumerate(items):
        cb4 = cb_next
        if idx + 1 < len(items):
            cb_next = cb_tile(*items[idx + 1])
        rs = slice(c * CHUNK, (c + 1) * CHUNK)
        gs = slice(g * gw, (g + 1) * gw)
        ns = slice(g * SSM_STATE, (g + 1) * SSM_STATE)
        cumx_c = cumx_s[rs, gs]
        cum_end = cumx_c[CHUNK - 1:CHUNK, :]
        xdt_c = xdt_s[rs, gs]
        st = state_ref[g]
        y_state = _dot(cact[rs, ns], st.astype(BF16)) * jnp.exp(cumx_c)
        cum_row = jnp.sum(jnp.where(diag, cumx_c, 0.0), axis=0, keepdims=True)
        seg = jnp.where(causal, cumx_c - cum_row, -jnp.inf)
        w4 = (jnp.exp(seg) * cb4).astype(BF16)
        x4 = jnp.concatenate([xdt_c.astype(BF16)] * hpg, axis=0)
        x4 = jnp.where(blockdiag, x4, jnp.zeros_like(x4))
        yacc[rs, gs] = _dot(w4, x4) + y_state
        xw = (xdt_c * jnp.exp(cum_end - cumx_c)).astype(BF16)
        state_ref[g] = st * jnp.exp(cum_end) + _dot_tn(bact[rs, ns], xw)

    for g in range(SSM_GROUPS):
        gs = slice(g * gw, (g + 1) * gw)
        y = yacc[:, gs] + dsk_ref[:, gs] * xact[:, gs]
        y = y * zg_ref[:, gs]
        ms = jnp.mean(y * y, axis=-1, keepdims=True)
        y_ref[:, gs] = (y * lax.rsqrt(ms + RMS_EPS) * nw_ref[:, gs]).astype(y_ref.dtype)


def _ssd(zx, dt_raw, conv_w, conv_b, dt_bias, a_log, d_skip, norm_w, expand, tril,
         *, batch, seq, d_inner, rows):
    t = zx.shape[0]
    n_conv = conv_w.shape[1]
    n_bc = (n_conv - d_inner) // 2
    nblk = seq // rows
    assert zx.shape[1] == d_inner + n_conv and n_conv == 2 * d_inner
    row_map = lambda col: (lambda b, s: (b * nblk + s, col))
    kern = functools.partial(_ssd_kernel, rows=rows, d_inner=d_inner, n_bc=n_bc)
    return pl.pallas_call(
        kern,
        grid=(batch, nblk),
        in_specs=[pl.BlockSpec((rows, d_inner), row_map(0)),
                  pl.BlockSpec((rows, d_inner), row_map(1)),
                  pl.BlockSpec((rows, d_inner), row_map(2)),
                  pl.BlockSpec((rows, LANES), row_map(0)),
                  _const_spec(conv_w.shape), _const_spec(conv_b.shape),
                  _const_spec(dt_bias.shape), _const_spec(a_log.shape),
                  _const_spec(d_skip.shape), _const_spec(norm_w.shape),
                  _const_spec(expand.shape), _const_spec(tril.shape)],
        out_specs=pl.BlockSpec((rows, d_inner), row_map(0)),
        out_shape=jax.ShapeDtypeStruct((t, d_inner), BF16),
        scratch_shapes=[pltpu.VMEM((n_conv // LANES, rows + 8, LANES), F32),
                        pltpu.VMEM((n_conv // LANES, rows, LANES), F32),
                        pltpu.VMEM((rows, d_inner), F32),
                        pltpu.VMEM((rows, n_bc), BF16),
                        pltpu.VMEM((rows, n_bc), BF16),
                        pltpu.VMEM((rows, d_inner), F32),
                        pltpu.VMEM((rows, d_inner), F32),
                        pltpu.VMEM((rows, d_inner), F32),
                        pltpu.VMEM((SSM_GROUPS, SSM_STATE, d_inner // SSM_GROUPS), F32)],
        compiler_params=_cparams("parallel", "arbitrary"),
        name="ssd_scan",
    )(zx, zx, zx, dt_raw, conv_w, conv_b, dt_bias, a_log, d_skip, norm_w, expand, tril)


def _proj_ln_kernel(a_ref, w_ref, x_ref, g_ref, b_ref, o_ref, *, alpha, parts):
    rows = a_ref.shape[0] // parts
    for r in range(parts):
        rs = slice(r * rows, (r + 1) * rows)
        h = _dot(a_ref[rs, :], w_ref[...])
        o_ref[rs, :] = _layer_norm(alpha * x_ref[rs, :] + h, g_ref[...], b_ref[...])


def _proj_ln(a, w, x2d, g, b, *, alpha, tm=512, parts=4):
    t, k = a.shape
    d = w.shape[1]
    return pl.pallas_call(
        functools.partial(_proj_ln_kernel, alpha=alpha, parts=parts),
        grid=(t // tm,),
        in_specs=[pl.BlockSpec((tm, k), lambda i: (i, 0)),
                  _const_spec((k, d)),
                  pl.BlockSpec((tm, d), lambda i: (i, 0)),
                  _const_spec((1, d)), _const_spec((1, d))],
        out_specs=pl.BlockSpec((tm, d), lambda i: (i, 0)),
        out_shape=jax.ShapeDtypeStruct((t, d), F32),
        compiler_params=_cparams("parallel"),
        name="proj_ln",
    )(a, w, x2d, g, b)


def _ffn_kernel(x_ref, wg_ref, wu_ref, wd_ref, g_ref, b_ref, o_ref, a_scr, *, alpha, tf):
    x = x_ref[...]
    xb = x.astype(BF16)
    d_ff = wd_ref.shape[0]
    for c in range(d_ff // tf):
        cs = slice(c * tf, (c + 1) * tf)
        a_scr[:, cs] = (_silu(_dot(xb, wg_ref[:, cs])) * _dot(xb, wu_ref[:, cs])).astype(BF16)
    h = _dot(a_scr[...], wd_ref[...])
    o_ref[...] = _layer_norm(alpha * x + h, g_ref[...], b_ref[...])


def _ffn(x2d, wg, wu, wd, g, b, *, alpha, tm=512, tf=256):
    t, d = x2d.shape
    d_ff = wd.shape[0]
    assert d_ff % tf == 0
    return pl.pallas_call(
        functools.partial(_ffn_kernel, alpha=alpha, tf=tf),
        grid=(t // tm,),
        in_specs=[pl.BlockSpec((tm, d), lambda i: (i, 0)),
                  _const_spec((d, d_ff)), _const_spec((d, d_ff)),
                  _const_spec((d_ff, d)),
                  _const_spec((1, d)), _const_spec((1, d))],
        out_specs=pl.BlockSpec((tm, d), lambda i: (i, 0)),
        out_shape=jax.ShapeDtypeStruct((t, d), F32),
        scratch_shapes=[pltpu.VMEM((tm, d_ff), BF16)],
        compiler_params=_cparams("parallel"),
        name="ffn_ln",
    )(x2d, wg, wu, wd, g, b)


def _qkv_kernel(x_ref, w_ref, wvt_ref, wf_ref, bf_ref, pk_ref, cq_ref, tril_ref,
                q_ref, k_ref, vt_ref, carry, *, att_dim, n_heads, scale):
    @pl.when(pl.program_id(1) == 0)
    def _():
        carry[...] = jnp.zeros(carry.shape, F32)

    xb = x_ref[...].astype(BF16)
    tm = xb.shape[0]
    lane = lax.broadcasted_iota(jnp.int32, (tm, LANES), 1)

    logf = -_softplus(-(_dot(xb, wf_ref[...]) + bf_ref[...]))
    logf = jnp.where(lane < n_heads, logf, 0.0)
    fc = _dot_f32_rhs(tril_ref[...], logf) + carry[...]
    carry[...] = fc[tm - 1:tm, :]
    hi, mid, lo = _split3(fc)
    feats = (hi.astype(F32) + pltpu.roll(mid.astype(F32), n_heads, axis=1)
             + pltpu.roll(lo.astype(F32), 2 * n_heads, axis=1)).astype(BF16)
    k_aug = _dot(feats, pk_ref[...])

    low = lane < ATT_HEAD_DIM
    tn = 2 * LANES
    for c in range(att_dim // tn):
        q4 = _dot(xb, w_ref[:, c * tn:(c + 1) * tn]) * scale
        k4 = _dot(xb, w_ref[:, att_dim + c * tn:att_dim + (c + 1) * tn])
        for h in range(tn // ATT_HEAD_DIM):
            ps = slice((h // 2) * LANES, (h // 2 + 1) * LANES)
            qh, kh = q4[:, ps], k4[:, ps]
            if h % 2:
                qh, kh = (pltpu.roll(a, ATT_HEAD_DIM, axis=1) for a in (qh, kh))
            head = c * (tn // ATT_HEAD_DIM) + h
            hs = slice(head * LANES, (head + 1) * LANES)
            q_ref[:, hs] = jnp.where(low, qh, cq_ref[...]).astype(BF16)
            k_ref[:, hs] = jnp.where(low, kh, k_aug[:, hs]).astype(BF16)
    vt_ref[0] = _dot_nt(wvt_ref[...], xb).astype(BF16)


def _qkv(x2d, w, wvt, wf, bf, pk, cq, tril, *, batch, seq, att_dim, n_heads, scale, tm):
    t, d = x2d.shape
    nblk = seq // tm
    wide = n_heads * LANES
    row = lambda b, s: (b * nblk + s, 0)
    kern = functools.partial(_qkv_kernel, att_dim=att_dim, n_heads=n_heads, scale=scale)
    return pl.pallas_call(
        kern,
        grid=(batch, nblk),
        in_specs=[pl.BlockSpec((tm, d), row),
                  _const_spec(w.shape), _const_spec(wvt.shape), _const_spec(wf.shape), _const_spec(bf.shape),
                  _const_spec(pk.shape), _const_spec(cq.shape), _const_spec(tril.shape)],
        out_specs=[pl.BlockSpec((tm, wide), row), pl.BlockSpec((tm, wide), row),
                   pl.BlockSpec((1, att_dim, tm), lambda b, s: (b, 0, s))],
        out_shape=[jax.ShapeDtypeStruct((t, wide), BF16), jax.ShapeDtypeStruct((t, wide), BF16),
                   jax.ShapeDtypeStruct((batch, att_dim, seq), BF16)],
        scratch_shapes=[pltpu.VMEM((1, LANES), F32)],
        compiler_params=_cparams("parallel", "arbitrary"),
        name="qkv_proj",
    )(x2d, w, wvt, wf, bf, pk, cq, tril)


def _attn_kernel(q_ref, k_ref, vt_ref, o_ref, vaug, *, tq, tk, seq):
    d = ATT_HEAD_DIM
    ones = jnp.ones((SUM_ROWS, seq), BF16)
    for hh in range(2):
        vaug[hh, 0:d, :] = vt_ref[0, hh * d:(hh + 1) * d, :]
        vaug[hh, d:d + SUM_ROWS, :] = ones

    def n_key_tiles(qi):
        return -(-(qi + 1) * tq // tk)

    def first_query(qi, j):
        return max(0, j * tk - qi * tq) // QB * QB

    def scores(qi, j, hh):
        hs = slice(hh * LANES, (hh + 1) * LANES)
        q0 = first_query(qi, j)
        s = _dot_nt(k_ref[0, j * tk:(j + 1) * tk, hs], q_ref[0, qi * tq + q0:(qi + 1) * tq, hs])
        if (j + 1) * tk - 1 > qi * tq + q0:
            row = lax.broadcasted_iota(jnp.int32, s.shape, 0)
            col = lax.broadcasted_iota(jnp.int32, s.shape, 1)
            s = jnp.where(row + j * tk <= col + (qi * tq + q0), s, -jnp.inf)
        return s

    items = [(qi, j, hh) for qi in range(seq // tq) for j in range(n_key_tiles(qi)) for hh in range(2)]
    m_run, acc_run, outs = {}, {}, {}

    def softmax_part(s, qi, j, hh):
        b0 = first_query(qi, j) // QB
        ps, rescales = [], []
        for b in range(b0, tq // QB):
            s_b = s[:, (b - b0) * QB:(b - b0 + 1) * QB]
            cm = jnp.max(s_b, axis=0, keepdims=True)
            if j == 0:
                m_new, rescale = cm, None
            else:
                m_new = jnp.maximum(m_run[hh, b], cm)
                rescale = jnp.exp(m_run[hh, b] - m_new)
            m_run[hh, b] = m_new
            ps.append(jnp.exp(s_b - m_new).astype(BF16))
            rescales.append(rescale)
        return jnp.concatenate(ps, axis=1), rescales

    def value_part(p, rescales, qi, j, hh):
        b0 = first_query(qi, j) // QB
        pv = _dot(vaug[hh, :, j * tk:(j + 1) * tk], p)
        for b in range(b0, tq // QB):
            pv_b = pv[:, (b - b0) * QB:(b - b0 + 1) * QB]
            acc_run[hh, b] = pv_b if j == 0 else rescales[b - b0] * acc_run[hh, b] + pv_b
        if j == n_key_tiles(qi) - 1:
            acc = jnp.concatenate([acc_run[hh, b] for b in range(tq // QB)], axis=1)
            outs[hh] = acc[0:d] / acc[d:d + 1]
            if hh == 1:
                o_ref[0, qi * tq:(qi + 1) * tq, :] = (
                    jnp.concatenate([outs[0], outs[1]], axis=0).T.astype(o_ref.dtype))

    n = len(items)
    s_q = {0: scores(*items[0])}
    if n > 1:
        s_q[1] = scores(*items[1])
    p_q = {0: softmax_part(s_q.pop(0), *items[0])}
    for idx in range(n):
        if idx + 2 < n:
            s_q[idx + 2] = scores(*items[idx + 2])
        if idx + 1 < n:
            p_q[idx + 1] = softmax_part(s_q.pop(idx + 1), *items[idx + 1])
        value_part(*p_q.pop(idx), *items[idx])


def _attention(q, k, vt, *, batch, seq, n_heads, tq, tk):
    wide = n_heads * LANES
    q3, k3 = (a.reshape(batch, seq, wide) for a in (q, k))
    pair = lambda b, p: (b, 0, p)
    return pl.pallas_call(
        functools.partial(_attn_kernel, tq=tq, tk=tk, seq=seq),
        grid=(batch, n_heads // 2),
        in_specs=[pl.BlockSpec((1, seq, 2 * LANES), pair),
                  pl.BlockSpec((1, seq, 2 * LANES), pair),
                  pl.BlockSpec((1, 2 * ATT_HEAD_DIM, seq), lambda b, p: (b, p, 0))],
        out_specs=pl.BlockSpec((1, seq, LANES), pair),
        out_shape=jax.ShapeDtypeStruct((batch, seq, n_heads * ATT_HEAD_DIM), BF16),
        scratch_shapes=[pltpu.VMEM((2, ATT_HEAD_DIM + SUM_ROWS, seq), BF16)],
        compiler_params=_cparams("parallel", "parallel"),
        name="fox_attention",
    )(q3, k3, vt)


def _pad_lanes(a, n=LANES):
    return jnp.pad(a, [(0, 0)] * (a.ndim - 1) + [(0, n - a.shape[-1])])


def _block_tril(n, block):
    i = jnp.arange(n)
    return ((i[:, None] >= i[None, :]) & (i[:, None] // block == i[None, :] // block)).astype(BF16)


def kernel(x, ssm_in_w, ssm_conv_w, ssm_conv_b, ssm_dt_bias, ssm_a_log, ssm_d, ssm_norm_w, ssm_out_w,
           kv_w, kv_b_f, att_q_w, att_o_w, ffn_gate_w, ffn_up_w, ffn_down_w,
           ln_mix_g, ln_mix_b, ln_ffn_g, ln_ffn_b):
    batch, seq, d_model = x.shape
    depth = ffn_gate_w.shape[0]
    assert ssm_in_w.shape[0] == 1 and att_q_w.shape[0] == 1 and depth == 2
    alpha = (2.0 * depth) ** 0.25
    d_inner = ssm_out_w.shape[1]
    n_conv = ssm_conv_w.shape[2]
    att_dim = att_q_w.shape[2]
    n_att_heads = att_dim // ATT_HEAD_DIM
    tf = 256
    ssd_rows = 256
    t = batch * seq
    x2d = x.reshape(t, d_model)
    row2 = lambda a: a.reshape(1, -1).astype(F32)

    in_w = ssm_in_w[0]
    n_main = d_inner + n_conv
    zx, dt_raw = _inproj(x2d, in_w[:, :n_main].astype(BF16), _pad_lanes(in_w[:, n_main:]).astype(BF16),
                         n_gate=d_inner)
    head_of_lane = jnp.arange(d_inner) // SSM_HEAD_DIM
    expand = (jnp.arange(LANES)[:, None] == head_of_lane[None, :]).astype(BF16)
    y = _ssd(zx, dt_raw, ssm_conv_w[0], row2(ssm_conv_b[0]),
             _pad_lanes(row2(ssm_dt_bias[0])), _pad_lanes(row2(ssm_a_log[0])),
             row2(jnp.repeat(ssm_d[0], SSM_HEAD_DIM)), row2(ssm_norm_w[0]),
             expand, _block_tril(ssd_rows, CHUNK),
             batch=batch, seq=seq, d_inner=d_inner, rows=ssd_rows)
    x2d = _proj_ln(y, ssm_out_w[0].astype(BF16), x2d, row2(ln_mix_g[0]), row2(ln_mix_b[0]), alpha=alpha)
    x2d = _ffn(x2d, ffn_gate_w[0].astype(BF16), ffn_up_w[0].astype(BF16),
               ffn_down_w[0].astype(BF16), row2(ln_ffn_g[0]), row2(ln_ffn_b[0]), alpha=alpha, tf=tf)

    wqk = jnp.concatenate([att_q_w[0], kv_w[:, :att_dim]], axis=1).astype(BF16)
    wvt = kv_w[:, att_dim:2 * att_dim].T.astype(BF16)
    wf = _pad_lanes(kv_w[:, 2 * att_dim:]).astype(BF16)
    wide = n_att_heads * LANES
    feat = jnp.arange(LANES)[:, None]
    slot = jnp.arange(wide)[None, :]
    pk = sum(-((feat == part * n_att_heads + h) & (slot == h * LANES + AUG + part)).astype(F32)
             for part in range(3) for h in range(n_att_heads)).astype(BF16)
    lane = jnp.arange(LANES)[None, :]
    cq = ((lane >= AUG) & (lane < AUG + 3)).astype(F32)
    tm_qkv = 512
    q, k, vt = _qkv(x2d, wqk, wvt, wf, _pad_lanes(row2(kv_b_f)), pk, cq,
                    _block_tril(tm_qkv, tm_qkv),
                    batch=batch, seq=seq, att_dim=att_dim, n_heads=n_att_heads,
                    scale=ATT_HEAD_DIM ** -0.5, tm=tm_qkv)

    o = _attention(q, k, vt, batch=batch, seq=seq, n_heads=n_att_heads, tq=1024, tk=256)
    x2d = _proj_ln(o.reshape(t, att_dim), att_o_w[0].astype(BF16), x2d,
                   row2(ln_mix_g[1]), row2(ln_mix_b[1]), alpha=alpha)
    x2d = _ffn(x2d, ffn_gate_w[1].astype(BF16), ffn_up_w[1].astype(BF16),
               ffn_down_w[1].astype(BF16), row2(ln_ffn_g[1]), row2(ln_ffn_b[1]), alpha=alpha, tf=tf)
    return x2d.reshape(batch, seq, d_model)
```

```python
import functools

import jax
import jax.numpy as jnp
from jax import lax
from jax.experimental import pallas as pl
from jax.experimental.pallas import tpu as pltpu

F32 = jnp.float32
BF16 = jnp.bfloat16

LANES = 128
VMEM_LIMIT = 56 * 1024 * 1024

SSM_HEAD_DIM = 64
SSM_GROUPS = 8
SSM_STATE = 128
CONV_K = 4
CHUNK = 64
ATT_HEAD_DIM = 64
LN_EPS = 1e-5
RMS_EPS = 1e-5
LOG2E = 1.4426950408889634

AUG = ATT_HEAD_DIM
SUM_ROWS = 16
QB = 2 * LANES


def _cparams(*sem):
    return pltpu.CompilerParams(dimension_semantics=sem, vmem_limit_bytes=VMEM_LIMIT)


def _const_spec(shape):
    nd = len(shape)
    return pl.BlockSpec(shape, lambda *_: (0,) * nd, pipeline_mode=pl.Buffered(1))


def _split3(a):
    hi = a.astype(BF16)
    r = a - hi.astype(F32)
    mid = r.astype(BF16)
    lo = (r - mid.astype(F32)).astype(BF16)
    return hi, mid, lo


def _dot(a, b):
    return jnp.dot(a, b, preferred_element_type=F32)


def _dot_nt(a, b):
    return lax.dot_general(a, b, (((1,), (1,)), ((), ())), preferred_element_type=F32)


def _dot_tn(a, b):
    return lax.dot_general(a, b, (((0,), (0,)), ((), ())), preferred_element_type=F32)


def _dot_f32_rhs(a_exact, b):
    hi, mid, lo = _split3(b)
    return _dot(a_exact, hi) + _dot(a_exact, mid) + _dot(a_exact, lo)


def _softplus(x):
    return jnp.maximum(x, 0.0) + jnp.log1p(jnp.exp(-jnp.abs(x)))


def _silu(x):
    return x * jax.nn.sigmoid(x)


def _layer_norm(t, g, b):
    mu = jnp.mean(t, axis=-1, keepdims=True)
    d = t - mu
    var = jnp.mean(d * d, axis=-1, keepdims=True)
    return d * lax.rsqrt(var + LN_EPS) * g + b


def _inproj_kernel(x_ref, w_ref, wdt_ref, zx_ref, dt_ref, *, tn, n_gate):
    xb = x_ref[...].astype(BF16)
    n = w_ref.shape[1]
    for c in range(n // tn):
        r = _dot(xb, w_ref[:, c * tn:(c + 1) * tn])
        zx_ref[:, c * tn:(c + 1) * tn] = _silu(r) if c * tn < n_gate else r
    dt_ref[...] = _dot(xb, wdt_ref[...])


def _inproj(x2d, w, wdt, *, n_gate, tm=512, tn=512):
    t, d = x2d.shape
    n = w.shape[1]
    assert n_gate % tn == 0
    return pl.pallas_call(
        functools.partial(_inproj_kernel, tn=tn, n_gate=n_gate),
        grid=(t // tm,),
        in_specs=[pl.BlockSpec((tm, d), lambda i: (i, 0)),
                  _const_spec((d, n)),
                  _const_spec((d, LANES))],
        out_specs=[pl.BlockSpec((tm, n), lambda i: (i, 0)),
                   pl.BlockSpec((tm, LANES), lambda i: (i, 0))],
        out_shape=[jax.ShapeDtypeStruct((t, n), F32),
                   jax.ShapeDtypeStruct((t, LANES), F32)],
        compiler_params=_cparams("parallel"),
        name="in_proj",
    )(x2d, w, wdt)


def _ssd_kernel(zg_ref, xs_ref, bc_ref, dtr_ref, cw_ref, cb_ref, dtb_ref, alog_ref,
                dsk_ref, nw_ref, e_ref, tril_ref,
                y_ref,
                ubuf, stage, xact, bact, cact, yacc, cumx_s, xdt_s, state_ref, *, rows, d_inner, n_bc):
    @pl.when(pl.program_id(1) == 0)
    def _():
        ubuf[:, 0:8, :] = jnp.zeros((ubuf.shape[0], 8, LANES), F32)
        state_ref[...] = jnp.zeros(state_ref.shape, F32)

    half = rows // 2
    for sl in range(ubuf.shape[0]):
        cs = slice(sl * LANES, (sl + 1) * LANES)
        src = xs_ref if sl * LANES < d_inner else bc_ref
        ubuf[sl, 8:8 + rows, :] = src[:, (sl * LANES) % d_inner:(sl * LANES) % d_inner + LANES]
        win = {c: ubuf[sl, pl.ds(8 + c, half, stride=2), :] for c in range(1 - CONV_K, 2)}
        even = cb_ref[:, cs]
        odd = cb_ref[:, cs]
        for k in range(CONV_K):
            tap = cw_ref[k:k + 1, cs]
            even = even + tap * win[k - (CONV_K - 1)]
            odd = odd + tap * win[k - (CONV_K - 1) + 1]
        stage[sl, pl.ds(0, half, stride=2), :] = _silu(even)
        stage[sl, pl.ds(1, half, stride=2), :] = _silu(odd)
        ubuf[sl, 0:8, :] = ubuf[sl, rows:rows + 8, :]
        act = stage[sl]
        if sl * LANES < d_inner:
            xact[:, cs] = act
        elif sl * LANES < d_inner + n_bc:
            bact[:, sl * LANES - d_inner:(sl + 1) * LANES - d_inner] = act.astype(BF16)
        else:
            o = sl * LANES - d_inner - n_bc
            cact[:, o:o + LANES] = act.astype(BF16)

    dt = _softplus(dtr_ref[...] + dtb_ref[...])
    a_neg = -jnp.exp(alog_ref[...])
    cum = _dot_f32_rhs(tril_ref[...], dt * a_neg) * LOG2E
    dt3 = _split3(dt)
    cum3 = _split3(cum)
    expand = lambda parts, e: _dot(parts[0], e) + _dot(parts[1], e) + _dot(parts[2], e)

    gw = (d_inner // SSM_GROUPS)
    hpg = gw // SSM_HEAD_DIM
    nch = rows // CHUNK

    li = lax.broadcasted_iota(jnp.int32, (CHUNK, gw), 0)
    si = lax.broadcasted_iota(jnp.int32, (CHUNK, gw), 1) % SSM_HEAD_DIM
    causal = li >= si
    diag = li == si
    bi = lax.broadcasted_iota(jnp.int32, (hpg * CHUNK, gw), 0) // CHUNK
    bj = lax.broadcasted_iota(jnp.int32, (hpg * CHUNK, gw), 1) // SSM_HEAD_DIM
    blockdiag = bi == bj

    for g in range(SSM_GROUPS):
        gs = slice(g * gw, (g + 1) * gw)
        e_g = e_ref[:, gs]
        cumx_s[:, gs] = expand(cum3, e_g)
        xdt_s[:, gs] = xact[:, gs] * expand(dt3, e_g)

    def cb_tile(c, g):
        rs = slice(c * CHUNK, (c + 1) * CHUNK)
        ns = slice(g * SSM_STATE, (g + 1) * SSM_STATE)
        return _dot_nt(cact[rs, ns], jnp.concatenate([bact[rs, ns]] * hpg, axis=0))

    items = [(c, g) for c in range(nch) for g in range(SSM_GROUPS)]
    cb_next = cb_tile(*items[0])
    for idx, (c, g) in enumerate(items):
        cb4 = cb_next
        if idx + 1 < len(items):
            cb_next = cb_tile(*items[idx + 1])
        rs = slice(c * CHUNK, (c + 1) * CHUNK)
        gs = slice(g * gw, (g + 1) * gw)
        ns = slice(g * SSM_STATE, (g + 1) * SSM_STATE)
        cumx_c = cumx_s[rs, gs]
        cum_end = cumx_c[CHUNK - 1:CHUNK, :]
        xdt_c = xdt_s[rs, gs]
        st = state_ref[g]
        y_state = _dot(cact[rs, ns], st.astype(BF16)) * jnp.exp2(cumx_c)
        cum_row = jnp.sum(jnp.where(diag, cumx_c, 0.0), axis=0, keepdims=True)
        seg = jnp.where(causal, cumx_c - cum_row, -jnp.inf)
        w4 = (jnp.exp2(seg) * cb4).astype(BF16)
        x4 = jnp.concatenate([xdt_c.astype(BF16)] * hpg, axis=0)
        x4 = jnp.where(blockdiag, x4, jnp.zeros_like(x4))
        yacc[rs, gs] = _dot(w4, x4) + y_state
        xw = (xdt_c * jnp.exp2(cum_end - cumx_c)).astype(BF16)
        state_ref[g] = st * jnp.exp2(cum_end) + _dot_tn(bact[rs, ns], xw)

    for g in range(SSM_GROUPS):
        gs = slice(g * gw, (g + 1) * gw)
        y = yacc[:, gs] + dsk_ref[:, gs] * xact[:, gs]
        y = y * zg_ref[:, gs]
        ms = jnp.mean(y * y, axis=-1, keepdims=True)
        y_ref[:, gs] = (y * lax.rsqrt(ms + RMS_EPS) * nw_ref[:, gs]).astype(y_ref.dtype)


def _ssd(zx, dt_raw, conv_w, conv_b, dt_bias, a_log, d_skip, norm_w, expand, tril,
         *, batch, seq, d_inner, rows):
    t = zx.shape[0]
    n_conv = conv_w.shape[1]
    n_bc = (n_conv - d_inner) // 2
    nblk = seq // rows
    assert zx.shape[1] == d_inner + n_conv and n_conv == 2 * d_inner
    row_map = lambda col: (lambda b, s: (b * nblk + s, col))
    kern = functools.partial(_ssd_kernel, rows=rows, d_inner=d_inner, n_bc=n_bc)
    return pl.pallas_call(
        kern,
        grid=(batch, nblk),
        in_specs=[pl.BlockSpec((rows, d_inner), row_map(0)),
                  pl.BlockSpec((rows, d_inner), row_map(1)),
                  pl.BlockSpec((rows, d_inner), row_map(2)),
                  pl.BlockSpec((rows, LANES), row_map(0)),
                  _const_spec(conv_w.shape), _const_spec(conv_b.shape),
                  _const_spec(dt_bias.shape), _const_spec(a_log.shape),
                  _const_spec(d_skip.shape), _const_spec(norm_w.shape),
                  _const_spec(expand.shape), _const_spec(tril.shape)],
        out_specs=pl.BlockSpec((rows, d_inner), row_map(0)),
        out_shape=jax.ShapeDtypeStruct((t, d_inner), BF16),
        scratch_shapes=[pltpu.VMEM((n_conv // LANES, rows + 8, LANES), F32),
                        pltpu.VMEM((n_conv // LANES, rows, LANES), F32),
                        pltpu.VMEM((rows, d_inner), F32),
                        pltpu.VMEM((rows, n_bc), BF16),
                        pltpu.VMEM((rows, n_bc), BF16),
                        pltpu.VMEM((rows, d_inner), F32),
                        pltpu.VMEM((rows, d_inner), F32),
                        pltpu.VMEM((rows, d_inner), F32),
                        pltpu.VMEM((SSM_GROUPS, SSM_STATE, d_inner // SSM_GROUPS), F32)],
        compiler_params=_cparams("parallel", "arbitrary"),
        name="ssd_scan",
    )(zx, zx, zx, dt_raw, conv_w, conv_b, dt_bias, a_log, d_skip, norm_w, expand, tril)


def _proj_ln_kernel(a_ref, w_ref, x_ref, g_ref, b_ref, o_ref, *, alpha, parts):
    rows = a_ref.shape[0] // parts
    for r in range(parts):
        rs = slice(r * rows, (r + 1) * rows)
        h = _dot(a_ref[rs, :], w_ref[...])
        o_ref[rs, :] = _layer_norm(alpha * x_ref[rs, :] + h, g_ref[...], b_ref[...])


def _proj_ln(a, w, x2d, g, b, *, alpha, tm=1024, parts=8):
    t, k = a.shape
    d = w.shape[1]
    return pl.pallas_call(
        functools.partial(_proj_ln_kernel, alpha=alpha, parts=parts),
        grid=(t // tm,),
        in_specs=[pl.BlockSpec((tm, k), lambda i: (i, 0)),
                  _const_spec((k, d)),
                  pl.BlockSpec((tm, d), lambda i: (i, 0)),
                  _const_spec((1, d)), _const_spec((1, d))],
        out_specs=pl.BlockSpec((tm, d), lambda i: (i, 0)),
        out_shape=jax.ShapeDtypeStruct((t, d), F32),
        compiler_params=_cparams("parallel"),
        name="proj_ln",
    )(a, w, x2d, g, b)


def _ffn_kernel(x_ref, wg_ref, wu_ref, wd_ref, g_ref, b_ref, o_ref, a_scr, *, alpha, tf, parts):
    xb = x_ref[...].astype(BF16)
    d_ff = wd_ref.shape[0]
    for c in range(d_ff // tf):
        cs = slice(c * tf, (c + 1) * tf)
        a_scr[:, cs] = (_silu(_dot(xb, wg_ref[:, cs])) * _dot(xb, wu_ref[:, cs])).astype(BF16)
    rows = x_ref.shape[0] // parts
    for r in range(parts):
        rs = slice(r * rows, (r + 1) * rows)
        h = _dot(a_scr[rs, :], wd_ref[...])
        o_ref[rs, :] = _layer_norm(alpha * x_ref[rs, :] + h, g_ref[...], b_ref[...])


def _ffn(x2d, wg, wu, wd, g, b, *, alpha, tm=1024, tf=256, parts=8):
    t, d = x2d.shape
    d_ff = wd.shape[0]
    assert d_ff % tf == 0
    return pl.pallas_call(
        functools.partial(_ffn_kernel, alpha=alpha, tf=tf, parts=parts),
        grid=(t // tm,),
        in_specs=[pl.BlockSpec((tm, d), lambda i: (i, 0)),
                  _const_spec((d, d_ff)), _const_spec((d, d_ff)),
                  _const_spec((d_ff, d)),
                  _const_spec((1, d)), _const_spec((1, d))],
        out_specs=pl.BlockSpec((tm, d), lambda i: (i, 0)),
        out_shape=jax.ShapeDtypeStruct((t, d), F32),
        scratch_shapes=[pltpu.VMEM((tm, d_ff), BF16)],
        compiler_params=_cparams("parallel"),
        name="ffn_ln",
    )(x2d, wg, wu, wd, g, b)


def _qkv_kernel(x_ref, w_ref, wvt_ref, wf_ref, bf_ref, pk_ref, cq_ref, tril_ref,
                q_ref, k_ref, vt_ref, carry, *, att_dim, n_heads, scale):
    @pl.when(pl.program_id(1) == 0)
    def _():
        carry[...] = jnp.zeros(carry.shape, F32)

    xb = x_ref[...].astype(BF16)
    tm = xb.shape[0]
    lane = lax.broadcasted_iota(jnp.int32, (tm, LANES), 1)

    logf = -_softplus(-(_dot(xb, wf_ref[...]) + bf_ref[...]))
    logf = jnp.where(lane < n_heads, logf, 0.0)
    fc = _dot_f32_rhs(tril_ref[...], logf) + carry[...]
    carry[...] = fc[tm - 1:tm, :]
    hi, mid, lo = _split3(fc * LOG2E)
    feats = (hi.astype(F32) + pltpu.roll(mid.astype(F32), n_heads, axis=1)
             + pltpu.roll(lo.astype(F32), 2 * n_heads, axis=1)).astype(BF16)
    k_aug = _dot(feats, pk_ref[...])

    low = lane < ATT_HEAD_DIM
    tn = 2 * LANES
    for c in range(att_dim // tn):
        q4 = _dot(xb, w_ref[:, c * tn:(c + 1) * tn]) * scale
        k4 = _dot(xb, w_ref[:, att_dim + c * tn:att_dim + (c + 1) * tn])
        for h in range(tn // ATT_HEAD_DIM):
            ps = slice((h // 2) * LANES, (h // 2 + 1) * LANES)
            qh, kh = q4[:, ps], k4[:, ps]
            if h % 2:
                qh, kh = (pltpu.roll(a, ATT_HEAD_DIM, axis=1) for a in (qh, kh))
            head = c * (tn // ATT_HEAD_DIM) + h
            hs = slice(head * LANES, (head + 1) * LANES)
            q_ref[:, hs] = jnp.where(low, qh, cq_ref[...]).astype(BF16)
            k_ref[:, hs] = jnp.where(low, kh, k_aug[:, hs]).astype(BF16)
    vt_ref[0] = _dot_nt(wvt_ref[...], xb).astype(BF16)


def _qkv(x2d, w, wvt, wf, bf, pk, cq, tril, *, batch, seq, att_dim, n_heads, scale, tm):
    t, d = x2d.shape
    nblk = seq // tm
    wide = n_heads * LANES
    row = lambda b, s: (b * nblk + s, 0)
    kern = functools.partial(_qkv_kernel, att_dim=att_dim, n_heads=n_heads, scale=scale)
    return pl.pallas_call(
        kern,
        grid=(batch, nblk),
        in_specs=[pl.BlockSpec((tm, d), row),
                  _const_spec(w.shape), _const_spec(wvt.shape), _const_spec(wf.shape), _const_spec(bf.shape),
                  _const_spec(pk.shape), _const_spec(cq.shape), _const_spec(tril.shape)],
        out_specs=[pl.BlockSpec((tm, wide), row), pl.BlockSpec((tm, wide), row),
                   pl.BlockSpec((1, att_dim, tm), lambda b, s: (b, 0, s))],
        out_shape=[jax.ShapeDtypeStruct((t, wide), BF16), jax.ShapeDtypeStruct((t, wide), BF16),
                   jax.ShapeDtypeStruct((batch, att_dim, seq), BF16)],
        scratch_shapes=[pltpu.VMEM((1, LANES), F32)],
        compiler_params=_cparams("parallel", "arbitrary"),
        name="qkv_proj",
    )(x2d, w, wvt, wf, bf, pk, cq, tril)


def _attn_kernel(q_ref, k_ref, vt_ref, o_ref, vaug, *, tq, tk, seq):
    d = ATT_HEAD_DIM
    ones = jnp.ones((SUM_ROWS, seq), BF16)
    for hh in range(2):
        vaug[hh, 0:d, :] = vt_ref[0, hh * d:(hh + 1) * d, :]
        vaug[hh, d:d + SUM_ROWS, :] = ones

    def n_key_tiles(qi):
        return -(-(qi + 1) * tq // tk)

    def first_query(qi, j):
        return max(0, j * tk - qi * tq) // QB * QB

    def scores(qi, j, hh):
        hs = slice(hh * LANES, (hh + 1) * LANES)
        q0 = first_query(qi, j)
        s = _dot_nt(k_ref[0, j * tk:(j + 1) * tk, hs], q_ref[0, qi * tq + q0:(qi + 1) * tq, hs])
        if (j + 1) * tk - 1 > qi * tq + q0:
            row = lax.broadcasted_iota(jnp.int32, s.shape, 0)
            col = lax.broadcasted_iota(jnp.int32, s.shape, 1)
            s = jnp.where(row + j * tk <= col + (qi * tq + q0), s, -jnp.inf)
        return s

    items = [(qi, j, hh) for qi in range(seq // tq) for j in range(n_key_tiles(qi)) for hh in range(2)]
    m_run, acc_run, outs = {}, {}, {}

    def softmax_part(s, qi, j, hh):
        b0 = first_query(qi, j) // QB
        ps, rescales = [], []
        for b in range(b0, tq // QB):
            s_b = s[:, (b - b0) * QB:(b - b0 + 1) * QB]
            cm = jnp.max(s_b, axis=0, keepdims=True)
            if j == 0:
                m_new, rescale = cm, None
            else:
                m_new = jnp.maximum(m_run[hh, b], cm)
                rescale = jnp.exp2(m_run[hh, b] - m_new)
            m_run[hh, b] = m_new
            ps.append(jnp.exp2(s_b - m_new).astype(BF16))
            rescales.append(rescale)
        return jnp.concatenate(ps, axis=1), rescales

    def value_part(p, rescales, qi, j, hh):
        b0 = first_query(qi, j) // QB
        pv = _dot(vaug[hh, :, j * tk:(j + 1) * tk], p)
        for b in range(b0, tq // QB):
            pv_b = pv[:, (b - b0) * QB:(b - b0 + 1) * QB]
            acc_run[hh, b] = pv_b if j == 0 else rescales[b - b0] * acc_run[hh, b] + pv_b
        if j == n_key_tiles(qi) - 1:
            acc = jnp.concatenate([acc_run[hh, b] for b in range(tq // QB)], axis=1)
            outs[hh] = acc[0:d] / acc[d:d + 1]
            if hh == 1:
                o_ref[0, qi * tq:(qi + 1) * tq, :] = (
                    jnp.concatenate([outs[0], outs[1]], axis=0).T.astype(o_ref.dtype))

    n = len(items)
    s_q = {0: scores(*items[0])}
    if n > 1:
        s_q[1] = scores(*items[1])
    p_q = {0: softmax_part(s_q.pop(0), *items[0])}
    for idx in range(n):
        if idx + 2 < n:
            s_q[idx + 2] = scores(*items[idx + 2])
        if idx + 1 < n:
            p_q[idx + 1] = softmax_part(s_q.pop(idx + 1), *items[idx + 1])
        value_part(*p_q.pop(idx), *items[idx])


def _attention(q, k, vt, *, batch, seq, n_heads, tq, tk):
    wide = n_heads * LANES
    q3, k3 = (a.reshape(batch, seq, wide) for a in (q, k))
    pair = lambda b, p: (b, 0, p)
    return pl.pallas_call(
        functools.partial(_attn_kernel, tq=tq, tk=tk, seq=seq),
        grid=(batch, n_heads // 2),
        in_specs=[pl.BlockSpec((1, seq, 2 * LANES), pair),
                  pl.BlockSpec((1, seq, 2 * LANES), pair),
                  pl.BlockSpec((1, 2 * ATT_HEAD_DIM, seq), lambda b, p: (b, p, 0))],
        out_specs=pl.BlockSpec((1, seq, LANES), pair),
        out_shape=jax.ShapeDtypeStruct((batch, seq, n_heads * ATT_HEAD_DIM), BF16),
        scratch_shapes=[pltpu.VMEM((2, ATT_HEAD_DIM + SUM_ROWS, seq), BF16)],
        compiler_params=_cparams("parallel", "parallel"),
        name="fox_attention",
    )(q3, k3, vt)


def _pad_lanes(a, n=LANES):
    return jnp.pad(a, [(0, 0)] * (a.ndim - 1) + [(0, n - a.shape[-1])])


def _block_tril(n, block):
    i = jnp.arange(n)
    return ((i[:, None] >= i[None, :]) & (i[:, None] // block == i[None, :] // block)).astype(BF16)


def kernel(x, ssm_in_w, ssm_conv_w, ssm_conv_b, ssm_dt_bias, ssm_a_log, ssm_d, ssm_norm_w, ssm_out_w,
           kv_w, kv_b_f, att_q_w, att_o_w, ffn_gate_w, ffn_up_w, ffn_down_w,
           ln_mix_g, ln_mix_b, ln_ffn_g, ln_ffn_b):
    batch, seq, d_model = x.shape
    depth = ffn_gate_w.shape[0]
    assert ssm_in_w.shape[0] == 1 and att_q_w.shape[0] == 1 and depth == 2
    alpha = (2.0 * depth) ** 0.25
    d_inner = ssm_out_w.shape[1]
    n_conv = ssm_conv_w.shape[2]
    att_dim = att_q_w.shape[2]
    n_att_heads = att_dim // ATT_HEAD_DIM
    tf = 256
    ssd_rows = 256
    t = batch * seq
    x2d = x.reshape(t, d_model)
    row2 = lambda a: a.reshape(1, -1).astype(F32)

    in_w = ssm_in_w[0]
    n_main = d_inner + n_conv
    zx, dt_raw = _inproj(x2d, in_w[:, :n_main].astype(BF16), _pad_lanes(in_w[:, n_main:]).astype(BF16),
                         n_gate=d_inner)
    head_of_lane = jnp.arange(d_inner) // SSM_HEAD_DIM
    expand = (jnp.arange(LANES)[:, None] == head_of_lane[None, :]).astype(BF16)
    y = _ssd(zx, dt_raw, ssm_conv_w[0], row2(ssm_conv_b[0]),
             _pad_lanes(row2(ssm_dt_bias[0])), _pad_lanes(row2(ssm_a_log[0])),
             row2(jnp.repeat(ssm_d[0], SSM_HEAD_DIM)), row2(ssm_norm_w[0]),
             expand, _block_tril(ssd_rows, CHUNK),
             batch=batch, seq=seq, d_inner=d_inner, rows=ssd_rows)
    x2d = _proj_ln(y, ssm_out_w[0].astype(BF16), x2d, row2(ln_mix_g[0]), row2(ln_mix_b[0]), alpha=alpha)
    x2d = _ffn(x2d, ffn_gate_w[0].astype(BF16), ffn_up_w[0].astype(BF16),
               ffn_down_w[0].astype(BF16), row2(ln_ffn_g[0]), row2(ln_ffn_b[0]), alpha=alpha, tf=tf)

    wqk = jnp.concatenate([att_q_w[0], kv_w[:, :att_dim]], axis=1).astype(BF16)
    wvt = kv_w[:, att_dim:2 * att_dim].T.astype(BF16)
    wf = _pad_lanes(kv_w[:, 2 * att_dim:]).astype(BF16)
    wide = n_att_heads * LANES
    feat = jnp.arange(LANES)[:, None]
    slot = jnp.arange(wide)[None, :]
    pk = sum(-((feat == part * n_att_heads + h) & (slot == h * LANES + AUG + part)).astype(F32)
             for part in range(3) for h in range(n_att_heads)).astype(BF16)
    lane = jnp.arange(LANES)[None, :]
    cq = ((lane >= AUG) & (lane < AUG + 3)).astype(F32)
    tm_qkv = 512
    q, k, vt = _qkv(x2d, wqk, wvt, wf, _pad_lanes(row2(kv_b_f)), pk, cq,
                    _block_tril(tm_qkv, tm_qkv),
                    batch=batch, seq=seq, att_dim=att_dim, n_heads=n_att_heads,
                    scale=ATT_HEAD_DIM ** -0.5 * LOG2E, tm=tm_qkv)

    o = _attention(q, k, vt, batch=batch, seq=seq, n_heads=n_att_heads, tq=1024, tk=256)
    x2d = _proj_ln(o.reshape(t, att_dim), att_o_w[0].astype(BF16), x2d,
                   row2(ln_mix_g[1]), row2(ln_mix_b[1]), alpha=alpha)
    x2d = _ffn(x2d, ffn_gate_w[1].astype(BF16), ffn_up_w[1].astype(BF16),
               ffn_down_w[1].astype(BF16), row2(ln_ffn_g[1]), row2(ln_ffn_b[1]), alpha=alpha, tf=tf)
    return x2d.reshape(batch, seq, d_model)
```

```python
import functools

import jax
import jax.numpy as jnp
from jax import lax
from jax.experimental import pallas as pl
from jax.experimental.pallas import tpu as pltpu

F32 = jnp.float32
BF16 = jnp.bfloat16

LANES = 128
VMEM_LIMIT = 56 * 1024 * 1024

SSM_HEAD_DIM = 64
SSM_GROUPS = 8
SSM_STATE = 128
CONV_K = 4
CHUNK = 64
ATT_HEAD_DIM = 64
LN_EPS = 1e-5
RMS_EPS = 1e-5
LOG2E = 1.4426950408889634

AUG = ATT_HEAD_DIM
SUM_ROWS = 16
QB = 2 * LANES


def _cparams(*sem):
    return pltpu.CompilerParams(dimension_semantics=sem, vmem_limit_bytes=VMEM_LIMIT)


def _const_spec(shape):
    nd = len(shape)
    return pl.BlockSpec(shape, lambda *_: (0,) * nd, pipeline_mode=pl.Buffered(1))


def _split3(a):
    hi = a.astype(BF16)
    r = a - hi.astype(F32)
    mid = r.astype(BF16)
    lo = (r - mid.astype(F32)).astype(BF16)
    return hi, mid, lo


def _dot(a, b):
    return jnp.dot(a, b, preferred_element_type=F32)


def _dot_nt(a, b):
    return lax.dot_general(a, b, (((1,), (1,)), ((), ())), preferred_element_type=F32)


def _dot_tn(a, b):
    return lax.dot_general(a, b, (((0,), (0,)), ((), ())), preferred_element_type=F32)


def _dot_f32_rhs(a_exact, b):
    hi, mid, lo = _split3(b)
    return _dot(a_exact, hi) + _dot(a_exact, mid) + _dot(a_exact, lo)


def _softplus(x):
    return jnp.maximum(x, 0.0) + jnp.log1p(jnp.exp(-jnp.abs(x)))


def _silu(x):
    return x * jax.nn.sigmoid(x)


def _layer_norm(t, g, b):
    mu = jnp.mean(t, axis=-1, keepdims=True)
    d = t - mu
    var = jnp.mean(d * d, axis=-1, keepdims=True)
    return d * lax.rsqrt(var + LN_EPS) * g + b


def _inproj_kernel(x_ref, w_ref, wdt_ref, zx_ref, dt_ref, *, tn, n_gate):
    xb = x_ref[...].astype(BF16)
    n = w_ref.shape[1]
    for c in range(n // tn):
        r = _dot(xb, w_ref[:, c * tn:(c + 1) * tn])
        zx_ref[:, c * tn:(c + 1) * tn] = _silu(r) if c * tn < n_gate else r
    dt_ref[...] = _dot(xb, wdt_ref[...])


def _inproj(x2d, w, wdt, *, n_gate, tm=512, tn=512):
    t, d = x2d.shape
    n = w.shape[1]
    assert n_gate % tn == 0
    return pl.pallas_call(
        functools.partial(_inproj_kernel, tn=tn, n_gate=n_gate),
        grid=(t // tm,),
        in_specs=[pl.BlockSpec((tm, d), lambda i: (i, 0)),
                  _const_spec((d, n)),
                  _const_spec((d, LANES))],
        out_specs=[pl.BlockSpec((tm, n), lambda i: (i, 0)),
                   pl.BlockSpec((tm, LANES), lambda i: (i, 0))],
        out_shape=[jax.ShapeDtypeStruct((t, n), F32),
                   jax.ShapeDtypeStruct((t, LANES), F32)],
        compiler_params=_cparams("parallel"),
        name="in_proj",
    )(x2d, w, wdt)


def _ssd_kernel(zg_ref, xs_ref, bc_ref, dtr_ref, cw_ref, cb_ref, dtb_ref, alog_ref,
                dsk_ref, nw_ref, e_ref, tril_ref,
                y_ref,
                ubuf, stage, bact, cact, yacc, cumx_s, xdt_s, state_ref, *, rows, d_inner, n_bc):
    @pl.when(pl.program_id(1) == 0)
    def _():
        ubuf[:, 0:8, :] = jnp.zeros((ubuf.shape[0], 8, LANES), F32)
        state_ref[...] = jnp.zeros(state_ref.shape, F32)

    half = rows // 2
    for sl in range(ubuf.shape[0]):
        cs = slice(sl * LANES, (sl + 1) * LANES)
        src = xs_ref if sl * LANES < d_inner else bc_ref
        ubuf[sl, 8:8 + rows, :] = src[:, (sl * LANES) % d_inner:(sl * LANES) % d_inner + LANES]
        win = {c: ubuf[sl, pl.ds(8 + c, half, stride=2), :] for c in range(1 - CONV_K, 2)}
        even = cb_ref[:, cs]
        odd = cb_ref[:, cs]
        for k in range(CONV_K):
            tap = cw_ref[k:k + 1, cs]
            even = even + tap * win[k - (CONV_K - 1)]
            odd = odd + tap * win[k - (CONV_K - 1) + 1]
        stage[sl, pl.ds(0, half, stride=2), :] = _silu(even)
        stage[sl, pl.ds(1, half, stride=2), :] = _silu(odd)
        ubuf[sl, 0:8, :] = ubuf[sl, rows:rows + 8, :]
    for sl in range(d_inner // LANES, ubuf.shape[0]):
        o = sl * LANES - d_inner
        if o < n_bc:
            bact[:, o:o + LANES] = stage[sl].astype(BF16)
        else:
            cact[:, o - n_bc:o - n_bc + LANES] = stage[sl].astype(BF16)

    def xact(g):
        per = d_inner // SSM_GROUPS // LANES
        return jnp.concatenate([stage[g * per + i] for i in range(per)], axis=1)

    dt = _softplus(dtr_ref[...] + dtb_ref[...])
    a_neg = -jnp.exp(alog_ref[...])
    cum = _dot_f32_rhs(tril_ref[...], dt * a_neg) * LOG2E
    dt3 = _split3(dt)
    cum3 = _split3(cum)
    expand = lambda parts, e: _dot(parts[0], e) + _dot(parts[1], e) + _dot(parts[2], e)

    gw = (d_inner // SSM_GROUPS)
    hpg = gw // SSM_HEAD_DIM
    nch = rows // CHUNK

    li = lax.broadcasted_iota(jnp.int32, (CHUNK, gw), 0)
    si = lax.broadcasted_iota(jnp.int32, (CHUNK, gw), 1) % SSM_HEAD_DIM
    causal = li >= si
    diag = li == si
    bi = lax.broadcasted_iota(jnp.int32, (hpg * CHUNK, gw), 0) // CHUNK
    bj = lax.broadcasted_iota(jnp.int32, (hpg * CHUNK, gw), 1) // SSM_HEAD_DIM
    blockdiag = bi == bj

    for g in range(SSM_GROUPS):
        gs = slice(g * gw, (g + 1) * gw)
        e_g = e_ref[:, gs]
        cumx_s[:, gs] = expand(cum3, e_g)
        xdt_s[:, gs] = xact(g) * expand(dt3, e_g)

    def cb_tile(c, g):
        rs = slice(c * CHUNK, (c + 1) * CHUNK)
        ns = slice(g * SSM_STATE, (g + 1) * SSM_STATE)
        return _dot_nt(cact[rs, ns], jnp.concatenate([bact[rs, ns]] * hpg, axis=0))

    items = [(c, g) for c in range(nch) for g in range(SSM_GROUPS)]
    cb_next = cb_tile(*items[0])
    for idx, (c, g) in enumerate(items):
        cb4 = cb_next
        if idx + 1 < len(items):
            cb_next = cb_tile(*items[idx + 1])
        rs = slice(c * CHUNK, (c + 1) * CHUNK)
        gs = slice(g * gw, (g + 1) * gw)
        ns = slice(g * SSM_STATE, (g + 1) * SSM_STATE)
        cumx_c = cumx_s[rs, gs]
        cum_end = cumx_c[CHUNK - 1:CHUNK, :]
        xdt_c = xdt_s[rs, gs]
        st = state_ref[g]
        y_state = _dot(cact[rs, ns], st.astype(BF16)) * jnp.exp2(cumx_c)
        cum_row = jnp.sum(jnp.where(diag, cumx_c, 0.0), axis=0, keepdims=True)
        seg = jnp.where(causal, cumx_c - cum_row, -jnp.inf)
        w4 = (jnp.exp2(seg) * cb4).astype(BF16)
        x4 = jnp.concatenate([xdt_c.astype(BF16)] * hpg, axis=0)
        x4 = jnp.where(blockdiag, x4, jnp.zeros_like(x4))
        yacc[rs, gs] = _dot(w4, x4) + y_state
        xw = (xdt_c * jnp.exp2(cum_end - cumx_c)).astype(BF16)
        state_ref[g] = st * jnp.exp2(cum_end) + _dot_tn(bact[rs, ns], xw)

    for g in range(SSM_GROUPS):
        gs = slice(g * gw, (g + 1) * gw)
        y = yacc[:, gs] + dsk_ref[:, gs] * xact(g)
        y = y * zg_ref[:, gs]
        ms = jnp.mean(y * y, axis=-1, keepdims=True)
        y_ref[:, gs] = (y * lax.rsqrt(ms + RMS_EPS) * nw_ref[:, gs]).astype(y_ref.dtype)


def _ssd(zx, dt_raw, conv_w, conv_b, dt_bias, a_log, d_skip, norm_w, expand, tril,
         *, batch, seq, d_inner, rows):
    t = zx.shape[0]
    n_conv = conv_w.shape[1]
    n_bc = (n_conv - d_inner) // 2
    nblk = seq // rows
    assert zx.shape[1] == d_inner + n_conv and n_conv == 2 * d_inner
    row_map = lambda col: (lambda b, s: (b * nblk + s, col))
    kern = functools.partial(_ssd_kernel, rows=rows, d_inner=d_inner, n_bc=n_bc)
    return pl.pallas_call(
        kern,
        grid=(batch, nblk),
        in_specs=[pl.BlockSpec((rows, d_inner), row_map(0)),
                  pl.BlockSpec((rows, d_inner), row_map(1)),
                  pl.BlockSpec((rows, d_inner), row_map(2)),
                  pl.BlockSpec((rows, LANES), row_map(0)),
                  _const_spec(conv_w.shape), _const_spec(conv_b.shape),
                  _const_spec(dt_bias.shape), _const_spec(a_log.shape),
                  _const_spec(d_skip.shape), _const_spec(norm_w.shape),
                  _const_spec(expand.shape), _const_spec(tril.shape)],
        out_specs=pl.BlockSpec((rows, d_inner), row_map(0)),
        out_shape=jax.ShapeDtypeStruct((t, d_inner), BF16),
        scratch_shapes=[pltpu.VMEM((n_conv // LANES, rows + 8, LANES), F32),
                        pltpu.VMEM((n_conv // LANES, rows, LANES), F32),
                        pltpu.VMEM((rows, n_bc), BF16),
                        pltpu.VMEM((rows, n_bc), BF16),
                        pltpu.VMEM((rows, d_inner), F32),
                        pltpu.VMEM((rows, d_inner), F32),
                        pltpu.VMEM((rows, d_inner), F32),
                        pltpu.VMEM((SSM_GROUPS, SSM_STATE, d_inner // SSM_GROUPS), F32)],
        compiler_params=_cparams("parallel", "arbitrary"),
        name="ssd_scan",
    )(zx, zx, zx, dt_raw, conv_w, conv_b, dt_bias, a_log, d_skip, norm_w, expand, tril)


def _proj_ln_kernel(a_ref, w_ref, x_ref, g_ref, b_ref, o_ref, *, alpha, parts):
    rows = a_ref.shape[0] // parts
    for r in range(parts):
        rs = slice(r * rows, (r + 1) * rows)
        h = _dot(a_ref[rs, :], w_ref[...])
        o_ref[rs, :] = _layer_norm(alpha * x_ref[rs, :] + h, g_ref[...], b_ref[...])


def _proj_ln(a, w, x2d, g, b, *, alpha, tm=1024, parts=8):
    t, k = a.shape
    d = w.shape[1]
    return pl.pallas_call(
        functools.partial(_proj_ln_kernel, alpha=alpha, parts=parts),
        grid=(t // tm,),
        in_specs=[pl.BlockSpec((tm, k), lambda i: (i, 0)),
                  _const_spec((k, d)),
                  pl.BlockSpec((tm, d), lambda i: (i, 0)),
                  _const_spec((1, d)), _const_spec((1, d))],
        out_specs=pl.BlockSpec((tm, d), lambda i: (i, 0)),
        out_shape=jax.ShapeDtypeStruct((t, d), F32),
        compiler_params=_cparams("parallel"),
        name="proj_ln",
    )(a, w, x2d, g, b)


def _ffn_kernel(x_ref, wg_ref, wu_ref, wd_ref, g_ref, b_ref, o_ref, a_scr, *, alpha, tf, parts):
    xb = x_ref[...].astype(BF16)
    d_ff = wd_ref.shape[0]
    for c in range(d_ff // tf):
        cs = slice(c * tf, (c + 1) * tf)
        a_scr[:, cs] = (_silu(_dot(xb, wg_ref[:, cs])) * _dot(xb, wu_ref[:, cs])).astype(BF16)
    rows = x_ref.shape[0] // parts
    for r in range(parts):
        rs = slice(r * rows, (r + 1) * rows)
        h = _dot(a_scr[rs, :], wd_ref[...])
        o_ref[rs, :] = _layer_norm(alpha * x_ref[rs, :] + h, g_ref[...], b_ref[...])


def _ffn(x2d, wg, wu, wd, g, b, *, alpha, tm=1024, tf=256, parts=8):
    t, d = x2d.shape
    d_ff = wd.shape[0]
    assert d_ff % tf == 0
    return pl.pallas_call(
        functools.partial(_ffn_kernel, alpha=alpha, tf=tf, parts=parts),
        grid=(t // tm,),
        in_specs=[pl.BlockSpec((tm, d), lambda i: (i, 0)),
                  _const_spec((d, d_ff)), _const_spec((d, d_ff)),
                  _const_spec((d_ff, d)),
                  _const_spec((1, d)), _const_spec((1, d))],
        out_specs=pl.BlockSpec((tm, d), lambda i: (i, 0)),
        out_shape=jax.ShapeDtypeStruct((t, d), F32),
        scratch_shapes=[pltpu.VMEM((tm, d_ff), BF16)],
        compiler_params=_cparams("parallel"),
        name="ffn_ln",
    )(x2d, wg, wu, wd, g, b)


def _qkv_kernel(x_ref, w_ref, wvt_ref, wf_ref, bf_ref, pk_ref, cq_ref, tril_ref,
                q_ref, k_ref, vt_ref, carry, *, att_dim, n_heads, scale):
    @pl.when(pl.program_id(1) == 0)
    def _():
        carry[...] = jnp.zeros(carry.shape, F32)

    xb = x_ref[...].astype(BF16)
    tm = xb.shape[0]
    lane = lax.broadcasted_iota(jnp.int32, (tm, LANES), 1)

    logf = -_softplus(-(_dot(xb, wf_ref[...]) + bf_ref[...]))
    logf = jnp.where(lane < n_heads, logf, 0.0)
    fc = _dot_f32_rhs(tril_ref[...], logf) + carry[...]
    carry[...] = fc[tm - 1:tm, :]
    hi, mid, lo = _split3(fc * LOG2E)
    feats = (hi.astype(F32) + pltpu.roll(mid.astype(F32), n_heads, axis=1)
             + pltpu.roll(lo.astype(F32), 2 * n_heads, axis=1)).astype(BF16)
    k_aug = _dot(feats, pk_ref[...])

    low = lane < ATT_HEAD_DIM
    tn = 2 * LANES
    for c in range(att_dim // tn):
        q4 = _dot(xb, w_ref[:, c * tn:(c + 1) * tn]) * scale
        k4 = _dot(xb, w_ref[:, att_dim + c * tn:att_dim + (c + 1) * tn])
        for h in range(tn // ATT_HEAD_DIM):
            ps = slice((h // 2) * LANES, (h // 2 + 1) * LANES)
            qh, kh = q4[:, ps], k4[:, ps]
            if h % 2:
                qh, kh = (pltpu.roll(a, ATT_HEAD_DIM, axis=1) for a in (qh, kh))
            head = c * (tn // ATT_HEAD_DIM) + h
            hs = slice(head * LANES, (head + 1) * LANES)
            q_ref[:, hs] = jnp.where(low, qh, cq_ref[...]).astype(BF16)
            k_ref[:, hs] = jnp.where(low, kh, k_aug[:, hs]).astype(BF16)
    vt_ref[0] = _dot_nt(wvt_ref[...], xb).astype(BF16)


def _qkv(x2d, w, wvt, wf, bf, pk, cq, tril, *, batch, seq, att_dim, n_heads, scale, tm):
    t, d = x2d.shape
    nblk = seq // tm
    wide = n_heads * LANES
    row = lambda b, s: (b * nblk + s, 0)
    kern = functools.partial(_qkv_kernel, att_dim=att_dim, n_heads=n_heads, scale=scale)
    return pl.pallas_call(
        kern,
        grid=(batch, nblk),
        in_specs=[pl.BlockSpec((tm, d), row),
                  _const_spec(w.shape), _const_spec(wvt.shape), _const_spec(wf.shape), _const_spec(bf.shape),
                  _const_spec(pk.shape), _const_spec(cq.shape), _const_spec(tril.shape)],
        out_specs=[pl.BlockSpec((tm, wide), row), pl.BlockSpec((tm, wide), row),
                   pl.BlockSpec((1, att_dim, tm), lambda b, s: (b, 0, s))],
        out_shape=[jax.ShapeDtypeStruct((t, wide), BF16), jax.ShapeDtypeStruct((t, wide), BF16),
                   jax.ShapeDtypeStruct((batch, att_dim, seq), BF16)],
        scratch_shapes=[pltpu.VMEM((1, LANES), F32)],
        compiler_params=_cparams("parallel", "arbitrary"),
        name="qkv_proj",
    )(x2d, w, wvt, wf, bf, pk, cq, tril)


def _attn_kernel(q_ref, k_ref, vt_ref, o_ref, vaug, *, tq, tk, seq):
    d = ATT_HEAD_DIM
    ones = jnp.ones((SUM_ROWS, seq), BF16)
    for hh in range(2):
        vaug[hh, 0:d, :] = vt_ref[0, hh * d:(hh + 1) * d, :]
        vaug[hh, d:d + SUM_ROWS, :] = ones

    def n_key_tiles(qi):
        return -(-(qi + 1) * tq // tk)

    def first_query(qi, j):
        return max(0, j * tk - qi * tq) // QB * QB

    def scores(qi, j, hh):
        hs = slice(hh * LANES, (hh + 1) * LANES)
        q0 = first_query(qi, j)
        s = _dot_nt(k_ref[0, j * tk:(j + 1) * tk, hs], q_ref[0, qi * tq + q0:(qi + 1) * tq, hs])
        if (j + 1) * tk - 1 > qi * tq + q0:
            row = lax.broadcasted_iota(jnp.int32, s.shape, 0)
            col = lax.broadcasted_iota(jnp.int32, s.shape, 1)
            s = jnp.where(row + j * tk <= col + (qi * tq + q0), s, -jnp.inf)
        return s

    items = [(qi, j, hh) for qi in range(seq // tq) for j in range(n_key_tiles(qi)) for hh in range(2)]
    m_run, acc_run, outs = {}, {}, {}

    def softmax_part(s, qi, j, hh):
        b0 = first_query(qi, j) // QB
        ps, rescales = [], []
        for b in range(b0, tq // QB):
            s_b = s[:, (b - b0) * QB:(b - b0 + 1) * QB]
            cm = jnp.max(s_b, axis=0, keepdims=True)
            if j == 0:
                m_new, rescale = cm, None
            else:
                m_new = jnp.maximum(m_run[hh, b], cm)
                rescale = jnp.exp2(m_run[hh, b] - m_new)
            m_run[hh, b] = m_new
            ps.append(jnp.exp2(s_b - m_new).astype(BF16))
            rescales.append(rescale)
        return jnp.concatenate(ps, axis=1), rescales

    def value_part(p, rescales, qi, j, hh):
        b0 = first_query(qi, j) // QB
        pv = _dot(vaug[hh, :, j * tk:(j + 1) * tk], p)
        for b in range(b0, tq // QB):
            pv_b = pv[:, (b - b0) * QB:(b - b0 + 1) * QB]
            acc_run[hh, b] = pv_b if j == 0 else rescales[b - b0] * acc_run[hh, b] + pv_b
        if j == n_key_tiles(qi) - 1:
            acc = jnp.concatenate([acc_run[hh, b] for b in range(tq // QB)], axis=1)
            outs[hh] = acc[0:d] / acc[d:d + 1]
            if hh == 1:
                o_ref[0, qi * tq:(qi + 1) * tq, :] = (
                    jnp.concatenate([outs[0], outs[1]], axis=0).T.astype(o_ref.dtype))

    n = len(items)
    s_q = {0: scores(*items[0])}
    if n > 1:
        s_q[1] = scores(*items[1])
    p_q = {0: softmax_part(s_q.pop(0), *items[0])}
    for idx in range(n):
        if idx + 2 < n:
            s_q[idx + 2] = scores(*items[idx + 2])
        if idx + 1 < n:
            p_q[idx + 1] = softmax_part(s_q.pop(idx + 1), *items[idx + 1])
        value_part(*p_q.pop(idx), *items[idx])


def _attention(q, k, vt, *, batch, seq, n_heads, tq, tk):
    wide = n_heads * LANES
    q3, k3 = (a.reshape(batch, seq, wide) for a in (q, k))
    pair = lambda b, p: (b, 0, p)
    return pl.pallas_call(
        functools.partial(_attn_kernel, tq=tq, tk=tk, seq=seq),
        grid=(batch, n_heads // 2),
        in_specs=[pl.BlockSpec((1, seq, 2 * LANES), pair),
                  pl.BlockSpec((1, seq, 2 * LANES), pair),
                  pl.BlockSpec((1, 2 * ATT_HEAD_DIM, seq), lambda b, p: (b, p, 0))],
        out_specs=pl.BlockSpec((1, seq, LANES), pair),
        out_shape=jax.ShapeDtypeStruct((batch, seq, n_heads * ATT_HEAD_DIM), BF16),
        scratch_shapes=[pltpu.VMEM((2, ATT_HEAD_DIM + SUM_ROWS, seq), BF16)],
        compiler_params=_cparams("parallel", "parallel"),
        name="fox_attention",
    )(q3, k3, vt)


def _pad_lanes(a, n=LANES):
    return jnp.pad(a, [(0, 0)] * (a.ndim - 1) + [(0, n - a.shape[-1])])


def _block_tril(n, block):
    i = jnp.arange(n)
    return ((i[:, None] >= i[None, :]) & (i[:, None] // block == i[None, :] // block)).astype(BF16)


def kernel(x, ssm_in_w, ssm_conv_w, ssm_conv_b, ssm_dt_bias, ssm_a_log, ssm_d, ssm_norm_w, ssm_out_w,
           kv_w, kv_b_f, att_q_w, att_o_w, ffn_gate_w, ffn_up_w, ffn_down_w,
           ln_mix_g, ln_mix_b, ln_ffn_g, ln_ffn_b):
    batch, seq, d_model = x.shape
    depth = ffn_gate_w.shape[0]
    assert ssm_in_w.shape[0] == 1 and att_q_w.shape[0] == 1 and depth == 2
    alpha = (2.0 * depth) ** 0.25
    d_inner = ssm_out_w.shape[1]
    n_conv = ssm_conv_w.shape[2]
    att_dim = att_q_w.shape[2]
    n_att_heads = att_dim // ATT_HEAD_DIM
    tf = 256
    ssd_rows = 256
    t = batch * seq
    x2d = x.reshape(t, d_model)
    row2 = lambda a: a.reshape(1, -1).astype(F32)

    in_w = ssm_in_w[0]
    n_main = d_inner + n_conv
    zx, dt_raw = _inproj(x2d, in_w[:, :n_main].astype(BF16), _pad_lanes(in_w[:, n_main:]).astype(BF16),
                         n_gate=d_inner)
    head_of_lane = jnp.arange(d_inner) // SSM_HEAD_DIM
    expand = (jnp.arange(LANES)[:, None] == head_of_lane[None, :]).astype(BF16)
    y = _ssd(zx, dt_raw, ssm_conv_w[0], row2(ssm_conv_b[0]),
             _pad_lanes(row2(ssm_dt_bias[0])), _pad_lanes(row2(ssm_a_log[0])),
             row2(jnp.repeat(ssm_d[0], SSM_HEAD_DIM)), row2(ssm_norm_w[0]),
             expand, _block_tril(ssd_rows, CHUNK),
             batch=batch, seq=seq, d_inner=d_inner, rows=ssd_rows)
    x2d = _proj_ln(y, ssm_out_w[0].astype(BF16), x2d, row2(ln_mix_g[0]), row2(ln_mix_b[0]), alpha=alpha)
    x2d = _ffn(x2d, ffn_gate_w[0].astype(BF16), ffn_up_w[0].astype(BF16),
               ffn_down_w[0].astype(BF16), row2(ln_ffn_g[0]), row2(ln_ffn_b[0]), alpha=alpha, tf=tf)

    wqk = jnp.concatenate([att_q_w[0], kv_w[:, :att_dim]], axis=1).astype(BF16)
    wvt = kv_w[:, att_dim:2 * att_dim].T.astype(BF16)
    wf = _pad_lanes(kv_w[:, 2 * att_dim:]).astype(BF16)
    wide = n_att_heads * LANES
    feat = jnp.arange(LANES)[:, None]
    slot = jnp.arange(wide)[None, :]
    pk = sum(-((feat == part * n_att_heads + h) & (slot == h * LANES + AUG + part)).astype(F32)
             for part in range(3) for h in range(n_att_heads)).astype(BF16)
    lane = jnp.arange(LANES)[None, :]
    cq = ((lane >= AUG) & (lane < AUG + 3)).astype(F32)
    tm_qkv = 512
    q, k, vt = _qkv(x2d, wqk, wvt, wf, _pad_lanes(row2(kv_b_f)), pk, cq,
                    _block_tril(tm_qkv, tm_qkv),
                    batch=batch, seq=seq, att_dim=att_dim, n_heads=n_att_heads,
                    scale=ATT_HEAD_DIM ** -0.5 * LOG2E, tm=tm_qkv)

    o = _attention(q, k, vt, batch=batch, seq=seq, n_heads=n_att_heads, tq=1024, tk=256)
    x2d = _proj_ln(o.reshape(t, att_dim), att_o_w[0].astype(BF16), x2d,
                   row2(ln_mix_g[1]), row2(ln_mix_b[1]), alpha=alpha)
    x2d = _ffn(x2d, ffn_gate_w[1].astype(BF16), ffn_up_w[1].astype(BF16),
               ffn_down_w[1].astype(BF16), row2(ln_ffn_g[1]), row2(ln_ffn_b[1]), alpha=alpha, tf=tf)
    return x2d.reshape(batch, seq, d_model)
```

```python
import functools

import jax
import jax.numpy as jnp
from jax import lax
from jax.experimental import pallas as pl
from jax.experimental.pallas import tpu as pltpu

F32 = jnp.float32
BF16 = jnp.bfloat16

LANES = 128
VMEM_LIMIT = 56 * 1024 * 1024

SSM_HEAD_DIM = 64
SSM_GROUPS = 8
SSM_STATE = 128
CONV_K = 4
CHUNK = 64
ATT_HEAD_DIM = 64
LN_EPS = 1e-5
RMS_EPS = 1e-5
LOG2E = 1.4426950408889634

AUG = ATT_HEAD_DIM
SUM_ROWS = 16
QB = 2 * LANES


def _cparams(*sem):
    return pltpu.CompilerParams(dimension_semantics=sem, vmem_limit_bytes=VMEM_LIMIT)


def _const_spec(shape):
    nd = len(shape)
    return pl.BlockSpec(shape, lambda *_: (0,) * nd, pipeline_mode=pl.Buffered(1))


def _split3(a):
    hi = a.astype(BF16)
    r = a - hi.astype(F32)
    mid = r.astype(BF16)
    lo = (r - mid.astype(F32)).astype(BF16)
    return hi, mid, lo


def _dot(a, b):
    return jnp.dot(a, b, preferred_element_type=F32)


def _dot_nt(a, b):
    return lax.dot_general(a, b, (((1,), (1,)), ((), ())), preferred_element_type=F32)


def _dot_tn(a, b):
    return lax.dot_general(a, b, (((0,), (0,)), ((), ())), preferred_element_type=F32)


def _dot_f32_rhs(a_exact, b):
    hi, mid, lo = _split3(b)
    return _dot(a_exact, hi) + _dot(a_exact, mid) + _dot(a_exact, lo)


def _softplus(x):
    return jnp.maximum(x, 0.0) + jnp.log1p(jnp.exp(-jnp.abs(x)))


def _silu(x):
    return x * jax.nn.sigmoid(x)


def _layer_norm(t, g, b):
    mu = jnp.mean(t, axis=-1, keepdims=True)
    d = t - mu
    var = jnp.mean(d * d, axis=-1, keepdims=True)
    return d * lax.rsqrt(var + LN_EPS) * g + b


def _inproj_kernel(x_ref, w_ref, wdt_ref, zx_ref, dt_ref, *, tn, n_gate):
    xb = x_ref[...].astype(BF16)
    n = w_ref.shape[1]
    for c in range(n // tn):
        r = _dot(xb, w_ref[:, c * tn:(c + 1) * tn])
        zx_ref[:, c * tn:(c + 1) * tn] = _silu(r) if c * tn < n_gate else r
    dt_ref[...] = _dot(xb, wdt_ref[...])


def _inproj(x2d, w, wdt, *, n_gate, tm=512, tn=512):
    t, d = x2d.shape
    n = w.shape[1]
    assert n_gate % tn == 0
    return pl.pallas_call(
        functools.partial(_inproj_kernel, tn=tn, n_gate=n_gate),
        grid=(t // tm,),
        in_specs=[pl.BlockSpec((tm, d), lambda i: (i, 0)),
                  _const_spec((d, n)),
                  _const_spec((d, LANES))],
        out_specs=[pl.BlockSpec((tm, n), lambda i: (i, 0)),
                   pl.BlockSpec((tm, LANES), lambda i: (i, 0))],
        out_shape=[jax.ShapeDtypeStruct((t, n), F32),
                   jax.ShapeDtypeStruct((t, LANES), F32)],
        compiler_params=_cparams("parallel"),
        name="in_proj",
    )(x2d, w, wdt)


def _ssd_kernel(zg_ref, xs_ref, bc_ref, dtr_ref, cw_ref, cb_ref, dtb_ref, alog_ref,
                dsk_ref, nw_ref, e_ref, tril_ref,
                y_ref,
                ubuf, stage, bact, cact, yacc, cumx_s, xdt_s, state_ref, *, rows, d_inner, n_bc):
    @pl.when(pl.program_id(1) == 0)
    def _():
        ubuf[:, 0:8, :] = jnp.zeros((ubuf.shape[0], 8, LANES), F32)
        state_ref[...] = jnp.zeros(state_ref.shape, F32)

    half = rows // 2
    for sl in range(ubuf.shape[0]):
        cs = slice(sl * LANES, (sl + 1) * LANES)
        src = xs_ref if sl * LANES < d_inner else bc_ref
        ubuf[sl, 8:8 + rows, :] = src[:, (sl * LANES) % d_inner:(sl * LANES) % d_inner + LANES]
        win = {c: ubuf[sl, pl.ds(8 + c, half, stride=2), :] for c in range(1 - CONV_K, 2)}
        even = cb_ref[:, cs]
        odd = cb_ref[:, cs]
        for k in range(CONV_K):
            tap = cw_ref[k:k + 1, cs]
            even = even + tap * win[k - (CONV_K - 1)]
            odd = odd + tap * win[k - (CONV_K - 1) + 1]
        stage[sl, pl.ds(0, half, stride=2), :] = _silu(even)
        stage[sl, pl.ds(1, half, stride=2), :] = _silu(odd)
        ubuf[sl, 0:8, :] = ubuf[sl, rows:rows + 8, :]
    for sl in range(d_inner // LANES, ubuf.shape[0]):
        o = sl * LANES - d_inner
        if o < n_bc:
            bact[:, o:o + LANES] = stage[sl].astype(BF16)
        else:
            cact[:, o - n_bc:o - n_bc + LANES] = stage[sl].astype(BF16)

    def xact(g):
        per = d_inner // SSM_GROUPS // LANES
        return jnp.concatenate([stage[g * per + i] for i in range(per)], axis=1)

    dt = _softplus(dtr_ref[...] + dtb_ref[...])
    a_neg = -jnp.exp(alog_ref[...])
    cum = _dot_f32_rhs(tril_ref[...], dt * a_neg) * LOG2E
    dt3 = _split3(dt)
    cum3 = _split3(cum)
    expand = lambda parts, e: _dot(parts[0], e) + _dot(parts[1], e) + _dot(parts[2], e)

    gw = (d_inner // SSM_GROUPS)
    hpg = gw // SSM_HEAD_DIM
    nch = rows // CHUNK

    li = lax.broadcasted_iota(jnp.int32, (CHUNK, gw), 0)
    si = lax.broadcasted_iota(jnp.int32, (CHUNK, gw), 1) % SSM_HEAD_DIM
    causal = li >= si
    diag = li == si
    lane_c = lax.broadcasted_iota(jnp.int32, (CHUNK, LANES), 1)
    zero_c = jnp.zeros((CHUNK, LANES), BF16)

    for g in range(SSM_GROUPS):
        gs = slice(g * gw, (g + 1) * gw)
        e_g = e_ref[:, gs]
        cumx_s[:, gs] = expand(cum3, e_g)
        xdt_s[:, gs] = xact(g) * expand(dt3, e_g)

    def cb_tile(c, g):
        rs = slice(c * CHUNK, (c + 1) * CHUNK)
        ns = slice(g * SSM_STATE, (g + 1) * SSM_STATE)
        return _dot_nt(cact[rs, ns], jnp.concatenate([bact[rs, ns]] * hpg, axis=0))

    items = [(c, g) for c in range(nch) for g in range(SSM_GROUPS)]
    cb_next = cb_tile(*items[0])
    for idx, (c, g) in enumerate(items):
        cb4 = cb_next
        if idx + 1 < len(items):
            cb_next = cb_tile(*items[idx + 1])
        rs = slice(c * CHUNK, (c + 1) * CHUNK)
        gs = slice(g * gw, (g + 1) * gw)
        ns = slice(g * SSM_STATE, (g + 1) * SSM_STATE)
        cumx_c = cumx_s[rs, gs]
        cum_end = cumx_c[CHUNK - 1:CHUNK, :]
        xdt_c = xdt_s[rs, gs]
        st = state_ref[g]
        y_state = _dot(cact[rs, ns], st.astype(BF16)) * jnp.exp2(cumx_c)
        cum_row = jnp.sum(jnp.where(diag, cumx_c, 0.0), axis=0, keepdims=True)
        seg = jnp.where(causal, cumx_c - cum_row, -jnp.inf)
        w4 = (jnp.exp2(seg) * cb4).astype(BF16)
        xdt_b = xdt_c.astype(BF16)
        blocks = []
        for h in range(hpg):
            v, lo = divmod(h * SSM_HEAD_DIM, LANES)
            piece = jnp.where((lane_c >= lo) & (lane_c < lo + SSM_HEAD_DIM),
                              xdt_b[:, v * LANES:(v + 1) * LANES], zero_c)
            blocks.append(jnp.concatenate([piece if j == v else zero_c for j in range(gw // LANES)], axis=1))
        yacc[rs, gs] = _dot(w4, jnp.concatenate(blocks, axis=0)) + y_state
        xw = (xdt_c * jnp.exp2(cum_end - cumx_c)).astype(BF16)
        state_ref[g] = st * jnp.exp2(cum_end) + _dot_tn(bact[rs, ns], xw)

    for g in range(SSM_GROUPS):
        gs = slice(g * gw, (g + 1) * gw)
        y = yacc[:, gs] + dsk_ref[:, gs] * xact(g)
        y = y * zg_ref[:, gs]
        ms = jnp.mean(y * y, axis=-1, keepdims=True)
        y_ref[:, gs] = (y * lax.rsqrt(ms + RMS_EPS) * nw_ref[:, gs]).astype(y_ref.dtype)


def _ssd(zx, dt_raw, conv_w, conv_b, dt_bias, a_log, d_skip, norm_w, expand, tril,
         *, batch, seq, d_inner, rows):
    t = zx.shape[0]
    n_conv = conv_w.shape[1]
    n_bc = (n_conv - d_inner) // 2
    nblk = seq // rows
    assert zx.shape[1] == d_inner + n_conv and n_conv == 2 * d_inner
    row_map = lambda col: (lambda b, s: (b * nblk + s, col))
    kern = functools.partial(_ssd_kernel, rows=rows, d_inner=d_inner, n_bc=n_bc)
    return pl.pallas_call(
        kern,
        grid=(batch, nblk),
        in_specs=[pl.BlockSpec((rows, d_inner), row_map(0)),
                  pl.BlockSpec((rows, d_inner), row_map(1)),
                  pl.BlockSpec((rows, d_inner), row_map(2)),
                  pl.BlockSpec((rows, LANES), row_map(0)),
                  _const_spec(conv_w.shape), _const_spec(conv_b.shape),
                  _const_spec(dt_bias.shape), _const_spec(a_log.shape),
                  _const_spec(d_skip.shape), _const_spec(norm_w.shape),
                  _const_spec(expand.shape), _const_spec(tril.shape)],
        out_specs=pl.BlockSpec((rows, d_inner), row_map(0)),
        out_shape=jax.ShapeDtypeStruct((t, d_inner), BF16),
        scratch_shapes=[pltpu.VMEM((n_conv // LANES, rows + 8, LANES), F32),
                        pltpu.VMEM((n_conv // LANES, rows, LANES), F32),
                        pltpu.VMEM((rows, n_bc), BF16),
                        pltpu.VMEM((rows, n_bc), BF16),
                        pltpu.VMEM((rows, d_inner), F32),
                        pltpu.VMEM((rows, d_inner), F32),
                        pltpu.VMEM((rows, d_inner), F32),
                        pltpu.VMEM((SSM_GROUPS, SSM_STATE, d_inner // SSM_GROUPS), F32)],
        compiler_params=_cparams("parallel", "arbitrary"),
        name="ssd_scan",
    )(zx, zx, zx, dt_raw, conv_w, conv_b, dt_bias, a_log, d_skip, norm_w, expand, tril)


def _proj_ln_kernel(a_ref, w_ref, x_ref, g_ref, b_ref, o_ref, *, alpha, parts):
    rows = a_ref.shape[0] // parts
    for r in range(parts):
        rs = slice(r * rows, (r + 1) * rows)
        h = _dot(a_ref[rs, :], w_ref[...])
        o_ref[rs, :] = _layer_norm(alpha * x_ref[rs, :] + h, g_ref[...], b_ref[...])


def _proj_ln(a, w, x2d, g, b, *, alpha, tm=1024, parts=4):
    t, k = a.shape
    d = w.shape[1]
    return pl.pallas_call(
        functools.partial(_proj_ln_kernel, alpha=alpha, parts=parts),
        grid=(t // tm,),
        in_specs=[pl.BlockSpec((tm, k), lambda i: (i, 0)),
                  _const_spec((k, d)),
                  pl.BlockSpec((tm, d), lambda i: (i, 0)),
                  _const_spec((1, d)), _const_spec((1, d))],
        out_specs=pl.BlockSpec((tm, d), lambda i: (i, 0)),
        out_shape=jax.ShapeDtypeStruct((t, d), F32),
        compiler_params=_cparams("parallel"),
        name="proj_ln",
    )(a, w, x2d, g, b)


def _ffn_kernel(x_ref, wg_ref, wu_ref, wd_ref, g_ref, b_ref, o_ref, a_scr, *, alpha, tf, parts):
    xb = x_ref[...].astype(BF16)
    d_ff = wd_ref.shape[0]
    for c in range(d_ff // tf):
        cs = slice(c * tf, (c + 1) * tf)
        a_scr[:, cs] = (_silu(_dot(xb, wg_ref[:, cs])) * _dot(xb, wu_ref[:, cs])).astype(BF16)
    rows = x_ref.shape[0] // parts
    for r in range(parts):
        rs = slice(r * rows, (r + 1) * rows)
        h = _dot(a_scr[rs, :], wd_ref[...])
        o_ref[rs, :] = _layer_norm(alpha * x_ref[rs, :] + h, g_ref[...], b_ref[...])


def _ffn(x2d, wg, wu, wd, g, b, *, alpha, tm=1024, tf=256, parts=4):
    t, d = x2d.shape
    d_ff = wd.shape[0]
    assert d_ff % tf == 0
    return pl.pallas_call(
        functools.partial(_ffn_kernel, alpha=alpha, tf=tf, parts=parts),
        grid=(t // tm,),
        in_specs=[pl.BlockSpec((tm, d), lambda i: (i, 0)),
                  _const_spec((d, d_ff)), _const_spec((d, d_ff)),
                  _const_spec((d_ff, d)),
                  _const_spec((1, d)), _const_spec((1, d))],
        out_specs=pl.BlockSpec((tm, d), lambda i: (i, 0)),
        out_shape=jax.ShapeDtypeStruct((t, d), F32),
        scratch_shapes=[pltpu.VMEM((tm, d_ff), BF16)],
        compiler_params=_cparams("parallel"),
        name="ffn_ln",
    )(x2d, wg, wu, wd, g, b)


def _qkv_kernel(x_ref, w_ref, wvt_ref, wf_ref, bf_ref, pk_ref, cq_ref, tril_ref,
                q_ref, k_ref, vt_ref, carry, *, att_dim, n_heads, scale):
    @pl.when(pl.program_id(1) == 0)
    def _():
        carry[...] = jnp.zeros(carry.shape, F32)

    xb = x_ref[...].astype(BF16)
    tm = xb.shape[0]
    lane = lax.broadcasted_iota(jnp.int32, (tm, LANES), 1)

    logf = -_softplus(-(_dot(xb, wf_ref[...]) + bf_ref[...]))
    logf = jnp.where(lane < n_heads, logf, 0.0)
    fc = _dot_f32_rhs(tril_ref[...], logf) + carry[...]
    carry[...] = fc[tm - 1:tm, :]
    hi, mid, lo = _split3(fc * LOG2E)
    feats = (hi.astype(F32) + pltpu.roll(mid.astype(F32), n_heads, axis=1)
             + pltpu.roll(lo.astype(F32), 2 * n_heads, axis=1)).astype(BF16)
    k_aug = _dot(feats, pk_ref[...])

    low = lane < ATT_HEAD_DIM
    tn = 2 * LANES
    for c in range(att_dim // tn):
        q4 = _dot(xb, w_ref[:, c * tn:(c + 1) * tn]) * scale
        k4 = _dot(xb, w_ref[:, att_dim + c * tn:att_dim + (c + 1) * tn])
        for h in range(tn // ATT_HEAD_DIM):
            ps = slice((h // 2) * LANES, (h // 2 + 1) * LANES)
            qh, kh = q4[:, ps], k4[:, ps]
            if h % 2:
                qh, kh = (pltpu.roll(a, ATT_HEAD_DIM, axis=1) for a in (qh, kh))
            head = c * (tn // ATT_HEAD_DIM) + h
            hs = slice(head * LANES, (head + 1) * LANES)
            q_ref[:, hs] = jnp.where(low, qh, cq_ref[...]).astype(BF16)
            k_ref[:, hs] = jnp.where(low, kh, k_aug[:, hs]).astype(BF16)
    vt_ref[0] = _dot_nt(wvt_ref[...], xb).astype(BF16)


def _qkv(x2d, w, wvt, wf, bf, pk, cq, tril, *, batch, seq, att_dim, n_heads, scale, tm):
    t, d = x2d.shape
    nblk = seq // tm
    wide = n_heads * LANES
    row = lambda b, s: (b * nblk + s, 0)
    kern = functools.partial(_qkv_kernel, att_dim=att_dim, n_heads=n_heads, scale=scale)
    return pl.pallas_call(
        kern,
        grid=(batch, nblk),
        in_specs=[pl.BlockSpec((tm, d), row),
                  _const_spec(w.shape), _const_spec(wvt.shape), _const_spec(wf.shape), _const_spec(bf.shape),
                  _const_spec(pk.shape), _const_spec(cq.shape), _const_spec(tril.shape)],
        out_specs=[pl.BlockSpec((tm, wide), row), pl.BlockSpec((tm, wide), row),
                   pl.BlockSpec((1, att_dim, tm), lambda b, s: (b, 0, s))],
        out_shape=[jax.ShapeDtypeStruct((t, wide), BF16), jax.ShapeDtypeStruct((t, wide), BF16),
                   jax.ShapeDtypeStruct((batch, att_dim, seq), BF16)],
        scratch_shapes=[pltpu.VMEM((1, LANES), F32)],
        compiler_params=_cparams("parallel", "arbitrary"),
        name="qkv_proj",
    )(x2d, w, wvt, wf, bf, pk, cq, tril)


def _attn_kernel(q_ref, k_ref, vt_ref, o_ref, vaug, *, tq, tk, seq):
    d = ATT_HEAD_DIM
    ones = jnp.ones((SUM_ROWS, seq), BF16)
    for hh in range(2):
        vaug[hh, 0:d, :] = vt_ref[0, hh * d:(hh + 1) * d, :]
        vaug[hh, d:d + SUM_ROWS, :] = ones

    def n_key_tiles(qi):
        return -(-(qi + 1) * tq // tk)

    def first_query(qi, j):
        return max(0, j * tk - qi * tq) // QB * QB

    def scores(qi, j, hh):
        hs = slice(hh * LANES, (hh + 1) * LANES)
        q0 = first_query(qi, j)
        s = _dot_nt(k_ref[0, j * tk:(j + 1) * tk, hs], q_ref[0, qi * tq + q0:(qi + 1) * tq, hs])
        if (j + 1) * tk - 1 > qi * tq + q0:
            row = lax.broadcasted_iota(jnp.int32, s.shape, 0)
            col = lax.broadcasted_iota(jnp.int32, s.shape, 1)
            s = jnp.where(row + j * tk <= col + (qi * tq + q0), s, -jnp.inf)
        return s

    items = [(qi, j, hh) for qi in range(seq // tq) for j in range(n_key_tiles(qi)) for hh in range(2)]
    m_run, acc_run, outs = {}, {}, {}

    def softmax_part(s, qi, j, hh):
        b0 = first_query(qi, j) // QB
        ps, rescales = [], []
        for b in range(b0, tq // QB):
            s_b = s[:, (b - b0) * QB:(b - b0 + 1) * QB]
            cm = jnp.max(s_b, axis=0, keepdims=True)
            if j == 0:
                m_new, rescale = cm, None
            else:
                m_new = jnp.maximum(m_run[hh, b], cm)
                rescale = jnp.exp2(m_run[hh, b] - m_new)
            m_run[hh, b] = m_new
            ps.append(jnp.exp2(s_b - m_new).astype(BF16))
            rescales.append(rescale)
        return jnp.concatenate(ps, axis=1), rescales

    def value_part(p, rescales, qi, j, hh):
        b0 = first_query(qi, j) // QB
        pv = _dot(vaug[hh, :, j * tk:(j + 1) * tk], p)
        for b in range(b0, tq // QB):
            pv_b = pv[:, (b - b0) * QB:(b - b0 + 1) * QB]
            acc_run[hh, b] = pv_b if j == 0 else rescales[b - b0] * acc_run[hh, b] + pv_b
        if j == n_key_tiles(qi) - 1:
            acc = jnp.concatenate([acc_run[hh, b] for b in range(tq // QB)], axis=1)
            outs[hh] = acc[0:d] / acc[d:d + 1]
            if hh == 1:
                o_ref[0, qi * tq:(qi + 1) * tq, :] = (
                    jnp.concatenate([outs[0], outs[1]], axis=0).T.astype(o_ref.dtype))

    n = len(items)
    s_q = {0: scores(*items[0])}
    if n > 1:
        s_q[1] = scores(*items[1])
    p_q = {0: softmax_part(s_q.pop(0), *items[0])}
    for idx in range(n):
        if idx + 2 < n:
            s_q[idx + 2] = scores(*items[idx + 2])
        if idx + 1 < n:
            p_q[idx + 1] = softmax_part(s_q.pop(idx + 1), *items[idx + 1])
        value_part(*p_q.pop(idx), *items[idx])


def _attention(q, k, vt, *, batch, seq, n_heads, tq, tk):
    wide = n_heads * LANES
    q3, k3 = (a.reshape(batch, seq, wide) for a in (q, k))
    pair = lambda b, p: (b, 0, p)
    return pl.pallas_call(
        functools.partial(_attn_kernel, tq=tq, tk=tk, seq=seq),
        grid=(batch, n_heads // 2),
        in_specs=[pl.BlockSpec((1, seq, 2 * LANES), pair),
                  pl.BlockSpec((1, seq, 2 * LANES), pair),
                  pl.BlockSpec((1, 2 * ATT_HEAD_DIM, seq), lambda b, p: (b, p, 0))],
        out_specs=pl.BlockSpec((1, seq, LANES), pair),
        out_shape=jax.ShapeDtypeStruct((batch, seq, n_heads * ATT_HEAD_DIM), BF16),
        scratch_shapes=[pltpu.VMEM((2, ATT_HEAD_DIM + SUM_ROWS, seq), BF16)],
        compiler_params=_cparams("parallel", "parallel"),
        name="fox_attention",
    )(q3, k3, vt)


def _pad_lanes(a, n=LANES):
    return jnp.pad(a, [(0, 0)] * (a.ndim - 1) + [(0, n - a.shape[-1])])


def _block_tril(n, block):
    i = jnp.arange(n)
    return ((i[:, None] >= i[None, :]) & (i[:, None] // block == i[None, :] // block)).astype(BF16)


def kernel(x, ssm_in_w, ssm_conv_w, ssm_conv_b, ssm_dt_bias, ssm_a_log, ssm_d, ssm_norm_w, ssm_out_w,
           kv_w, kv_b_f, att_q_w, att_o_w, ffn_gate_w, ffn_up_w, ffn_down_w,
           ln_mix_g, ln_mix_b, ln_ffn_g, ln_ffn_b):
    batch, seq, d_model = x.shape
    depth = ffn_gate_w.shape[0]
    assert ssm_in_w.shape[0] == 1 and att_q_w.shape[0] == 1 and depth == 2
    alpha = (2.0 * depth) ** 0.25
    d_inner = ssm_out_w.shape[1]
    n_conv = ssm_conv_w.shape[2]
    att_dim = att_q_w.shape[2]
    n_att_heads = att_dim // ATT_HEAD_DIM
    tf = 256
    ssd_rows = 256
    t = batch * seq
    x2d = x.reshape(t, d_model)
    row2 = lambda a: a.reshape(1, -1).astype(F32)

    in_w = ssm_in_w[0]
    n_main = d_inner + n_conv
    zx, dt_raw = _inproj(x2d, in_w[:, :n_main].astype(BF16), _pad_lanes(in_w[:, n_main:]).astype(BF16),
                         n_gate=d_inner)
    head_of_lane = jnp.arange(d_inner) // SSM_HEAD_DIM
    expand = (jnp.arange(LANES)[:, None] == head_of_lane[None, :]).astype(BF16)
    y = _ssd(zx, dt_raw, ssm_conv_w[0], row2(ssm_conv_b[0]),
             _pad_lanes(row2(ssm_dt_bias[0])), _pad_lanes(row2(ssm_a_log[0])),
             row2(jnp.repeat(ssm_d[0], SSM_HEAD_DIM)), row2(ssm_norm_w[0]),
             expand, _block_tril(ssd_rows, CHUNK),
             batch=batch, seq=seq, d_inner=d_inner, rows=ssd_rows)
    x2d = _proj_ln(y, ssm_out_w[0].astype(BF16), x2d, row2(ln_mix_g[0]), row2(ln_mix_b[0]), alpha=alpha)
    x2d = _ffn(x2d, ffn_gate_w[0].astype(BF16), ffn_up_w[0].astype(BF16),
               ffn_down_w[0].astype(BF16), row2(ln_ffn_g[0]), row2(ln_ffn_b[0]), alpha=alpha, tf=tf)

    wqk = jnp.concatenate([att_q_w[0], kv_w[:, :att_dim]], axis=1).astype(BF16)
    wvt = kv_w[:, att_dim:2 * att_dim].T.astype(BF16)
    wf = _pad_lanes(kv_w[:, 2 * att_dim:]).astype(BF16)
    wide = n_att_heads * LANES
    feat = jnp.arange(LANES)[:, None]
    slot = jnp.arange(wide)[None, :]
    pk = sum(-((feat == part * n_att_heads + h) & (slot == h * LANES + AUG + part)).astype(F32)
             for part in range(3) for h in range(n_att_heads)).astype(BF16)
    lane = jnp.arange(LANES)[None, :]
    cq = ((lane >= AUG) & (lane < AUG + 3)).astype(F32)
    tm_qkv = 512
    q, k, vt = _qkv(x2d, wqk, wvt, wf, _pad_lanes(row2(kv_b_f)), pk, cq,
                    _block_tril(tm_qkv, tm_qkv),
                    batch=batch, seq=seq, att_dim=att_dim, n_heads=n_att_heads,
                    scale=ATT_HEAD_DIM ** -0.5 * LOG2E, tm=tm_qkv)

    o = _attention(q, k, vt, batch=batch, seq=seq, n_heads=n_att_heads, tq=1024, tk=256)
    x2d = _proj_ln(o.reshape(t, att_dim), att_o_w[0].astype(BF16), x2d,
                   row2(ln_mix_g[1]), row2(ln_mix_b[1]), alpha=alpha)
    x2d = _ffn(x2d, ffn_gate_w[1].astype(BF16), ffn_up_w[1].astype(BF16),
               ffn_down_w[1].astype(BF16), row2(ln_ffn_g[1]), row2(ln_ffn_b[1]), alpha=alpha, tf=tf)
    return x2d.reshape(batch, seq, d_model)
```

```python
import functools

import jax
import jax.numpy as jnp
from jax import lax
from jax.experimental import pallas as pl
from jax.experimental.pallas import tpu as pltpu

F32 = jnp.float32
BF16 = jnp.bfloat16

LANES = 128
VMEM_LIMIT = 56 * 1024 * 1024

SSM_HEAD_DIM = 64
SSM_GROUPS = 8
SSM_STATE = 128
CONV_K = 4
CHUNK = 64
ATT_HEAD_DIM = 64
LN_EPS = 1e-5
RMS_EPS = 1e-5
LOG2E = 1.4426950408889634

AUG = ATT_HEAD_DIM
SUM_ROWS = 16
QB = 2 * LANES


def _cparams(*sem):
    return pltpu.CompilerParams(dimension_semantics=sem, vmem_limit_bytes=VMEM_LIMIT)


def _const_spec(shape):
    nd = len(shape)
    return pl.BlockSpec(shape, lambda *_: (0,) * nd, pipeline_mode=pl.Buffered(1))


def _split3(a):
    hi = a.astype(BF16)
    r = a - hi.astype(F32)
    mid = r.astype(BF16)
    lo = (r - mid.astype(F32)).astype(BF16)
    return hi, mid, lo


def _dot(a, b):
    return jnp.dot(a, b, preferred_element_type=F32)


def _dot_nt(a, b):
    return lax.dot_general(a, b, (((1,), (1,)), ((), ())), preferred_element_type=F32)


def _dot_tn(a, b):
    return lax.dot_general(a, b, (((0,), (0,)), ((), ())), preferred_element_type=F32)


def _dot_f32_rhs(a_exact, b):
    hi, mid, lo = _split3(b)
    return _dot(a_exact, hi) + _dot(a_exact, mid) + _dot(a_exact, lo)


def _softplus(x):
    return jnp.maximum(x, 0.0) + jnp.log1p(jnp.exp(-jnp.abs(x)))


def _silu(x):
    return x * jax.nn.sigmoid(x)


def _layer_norm(t, g, b):
    mu = jnp.mean(t, axis=-1, keepdims=True)
    d = t - mu
    var = jnp.mean(d * d, axis=-1, keepdims=True)
    return d * lax.rsqrt(var + LN_EPS) * g + b


def _inproj_kernel(x_ref, w_ref, wdt_ref, zx_ref, dt_ref, *, tn, n_gate):
    xb = x_ref[...].astype(BF16)
    n = w_ref.shape[1]
    for c in range(n // tn):
        r = _dot(xb, w_ref[:, c * tn:(c + 1) * tn])
        zx_ref[:, c * tn:(c + 1) * tn] = _silu(r) if c * tn < n_gate else r
    dt_ref[...] = _dot(xb, wdt_ref[...])


def _inproj(x2d, w, wdt, *, n_gate, tm=512, tn=512):
    t, d = x2d.shape
    n = w.shape[1]
    assert n_gate % tn == 0
    return pl.pallas_call(
        functools.partial(_inproj_kernel, tn=tn, n_gate=n_gate),
        grid=(t // tm,),
        in_specs=[pl.BlockSpec((tm, d), lambda i: (i, 0)),
                  _const_spec((d, n)),
                  _const_spec((d, LANES))],
        out_specs=[pl.BlockSpec((tm, n), lambda i: (i, 0)),
                   pl.BlockSpec((tm, LANES), lambda i: (i, 0))],
        out_shape=[jax.ShapeDtypeStruct((t, n), F32),
                   jax.ShapeDtypeStruct((t, LANES), F32)],
        compiler_params=_cparams("parallel"),
        name="in_proj",
    )(x2d, w, wdt)


def _ssd_kernel(zg_ref, xs_ref, bc_ref, dtr_ref, cw_ref, cb_ref, dtb_ref, alog_ref,
                dsk_ref, nw_ref, e_ref, tril_ref,
                y_ref,
                ubuf, stage, bact, cact, yacc, cumx_s, xdt_s, state_ref, *, rows, d_inner, n_bc):
    @pl.when(pl.program_id(1) == 0)
    def _():
        ubuf[:, 0:8, :] = jnp.zeros((ubuf.shape[0], 8, LANES), F32)
        state_ref[...] = jnp.zeros(state_ref.shape, F32)

    half = rows // 2
    for sl in range(ubuf.shape[0]):
        cs = slice(sl * LANES, (sl + 1) * LANES)
        src = xs_ref if sl * LANES < d_inner else bc_ref
        ubuf[sl, 8:8 + rows, :] = src[:, (sl * LANES) % d_inner:(sl * LANES) % d_inner + LANES]
        win = {c: ubuf[sl, pl.ds(8 + c, half, stride=2), :] for c in range(1 - CONV_K, 2)}
        even = cb_ref[:, cs]
        odd = cb_ref[:, cs]
        for k in range(CONV_K):
            tap = cw_ref[k:k + 1, cs]
            even = even + tap * win[k - (CONV_K - 1)]
            odd = odd + tap * win[k - (CONV_K - 1) + 1]
        stage[sl, pl.ds(0, half, stride=2), :] = _silu(even)
        stage[sl, pl.ds(1, half, stride=2), :] = _silu(odd)
        ubuf[sl, 0:8, :] = ubuf[sl, rows:rows + 8, :]
    for sl in range(d_inner // LANES, ubuf.shape[0]):
        o = sl * LANES - d_inner
        if o < n_bc:
            bact[:, o:o + LANES] = stage[sl].astype(BF16)
        else:
            cact[:, o - n_bc:o - n_bc + LANES] = stage[sl].astype(BF16)

    def xact(g):
        per = d_inner // SSM_GROUPS // LANES
        return jnp.concatenate([stage[g * per + i] for i in range(per)], axis=1)

    dt = _softplus(dtr_ref[...] + dtb_ref[...])
    a_neg = -jnp.exp(alog_ref[...])
    cum = _dot_f32_rhs(tril_ref[...], dt * a_neg) * LOG2E
    dt3 = _split3(dt)
    cum3 = _split3(cum)
    expand = lambda parts, e: _dot(parts[0], e) + _dot(parts[1], e) + _dot(parts[2], e)

    gw = (d_inner // SSM_GROUPS)
    hpg = gw // SSM_HEAD_DIM
    nch = rows // CHUNK

    li = lax.broadcasted_iota(jnp.int32, (CHUNK, gw), 0)
    si = lax.broadcasted_iota(jnp.int32, (CHUNK, gw), 1) % SSM_HEAD_DIM
    causal = li >= si
    diag = li == si
    lane_c = lax.broadcasted_iota(jnp.int32, (CHUNK, LANES), 1)
    zero_c = jnp.zeros((CHUNK, LANES), BF16)

    for g in range(SSM_GROUPS):
        gs = slice(g * gw, (g + 1) * gw)
        e_g = e_ref[:, gs]
        cumx_s[:, gs] = expand(cum3, e_g)
        xdt_s[:, gs] = xact(g) * expand(dt3, e_g)

    def cb_tile(c, g):
        rs = slice(c * CHUNK, (c + 1) * CHUNK)
        ns = slice(g * SSM_STATE, (g + 1) * SSM_STATE)
        return _dot_nt(cact[rs, ns], jnp.concatenate([bact[rs, ns]] * hpg, axis=0))

    items = [(c, g) for c in range(nch) for g in range(SSM_GROUPS)]
    cb_next = cb_tile(*items[0])
    for idx, (c, g) in enumerate(items):
        cb4 = cb_next
        if idx + 1 < len(items):
            cb_next = cb_tile(*items[idx + 1])
        rs = slice(c * CHUNK, (c + 1) * CHUNK)
        gs = slice(g * gw, (g + 1) * gw)
        ns = slice(g * SSM_STATE, (g + 1) * SSM_STATE)
        cumx_c = cumx_s[rs, gs]
        cum_end = cumx_c[CHUNK - 1:CHUNK, :]
        xdt_c = xdt_s[rs, gs]
        st = state_ref[g]
        y_state = _dot(cact[rs, ns], st.astype(BF16)) * jnp.exp2(cumx_c)
        cum_row = jnp.sum(jnp.where(diag, cumx_c, 0.0), axis=0, keepdims=True)
        seg = jnp.where(causal, cumx_c - cum_row, -jnp.inf)
        w4 = (jnp.exp2(seg) * cb4).astype(BF16)
        xdt_b = xdt_c.astype(BF16)
        blocks = []
        for h in range(hpg):
            v, lo = divmod(h * SSM_HEAD_DIM, LANES)
            piece = jnp.where((lane_c >= lo) & (lane_c < lo + SSM_HEAD_DIM),
                              xdt_b[:, v * LANES:(v + 1) * LANES], zero_c)
            blocks.append(jnp.concatenate([piece if j == v else zero_c for j in range(gw // LANES)], axis=1))
        yacc[rs, gs] = _dot(w4, jnp.concatenate(blocks, axis=0)) + y_state
        xw = (xdt_c * jnp.exp2(cum_end - cumx_c)).astype(BF16)
        state_ref[g] = st * jnp.exp2(cum_end) + _dot_tn(bact[rs, ns], xw)

    for g in range(SSM_GROUPS):
        gs = slice(g * gw, (g + 1) * gw)
        y = yacc[:, gs] + dsk_ref[:, gs] * xact(g)
        y = y * zg_ref[:, gs]
        ms = jnp.mean(y * y, axis=-1, keepdims=True)
        y_ref[:, gs] = (y * lax.rsqrt(ms + RMS_EPS) * nw_ref[:, gs]).astype(y_ref.dtype)


def _ssd(zx, dt_raw, conv_w, conv_b, dt_bias, a_log, d_skip, norm_w, expand, tril,
         *, batch, seq, d_inner, rows):
    t = zx.shape[0]
    n_conv = conv_w.shape[1]
    n_bc = (n_conv - d_inner) // 2
    nblk = seq // rows
    assert zx.shape[1] == d_inner + n_conv and n_conv == 2 * d_inner
    row_map = lambda col: (lambda b, s: (b * nblk + s, col))
    kern = functools.partial(_ssd_kernel, rows=rows, d_inner=d_inner, n_bc=n_bc)
    return pl.pallas_call(
        kern,
        grid=(batch, nblk),
        in_specs=[pl.BlockSpec((rows, d_inner), row_map(0)),
                  pl.BlockSpec((rows, d_inner), row_map(1)),
                  pl.BlockSpec((rows, d_inner), row_map(2)),
                  pl.BlockSpec((rows, LANES), row_map(0)),
                  _const_spec(conv_w.shape), _const_spec(conv_b.shape),
                  _const_spec(dt_bias.shape), _const_spec(a_log.shape),
                  _const_spec(d_skip.shape), _const_spec(norm_w.shape),
                  _const_spec(expand.shape), _const_spec(tril.shape)],
        out_specs=pl.BlockSpec((rows, d_inner), row_map(0)),
        out_shape=jax.ShapeDtypeStruct((t, d_inner), BF16),
        scratch_shapes=[pltpu.VMEM((n_conv // LANES, rows + 8, LANES), F32),
                        pltpu.VMEM((n_conv // LANES, rows, LANES), F32),
                        pltpu.VMEM((rows, n_bc), BF16),
                        pltpu.VMEM((rows, n_bc), BF16),
                        pltpu.VMEM((rows, d_inner), F32),
                        pltpu.VMEM((rows, d_inner), F32),
                        pltpu.VMEM((rows, d_inner), F32),
                        pltpu.VMEM((SSM_GROUPS, SSM_STATE, d_inner // SSM_GROUPS), F32)],
        compiler_params=_cparams("parallel", "arbitrary"),
        name="ssd_scan",
    )(zx, zx, zx, dt_raw, conv_w, conv_b, dt_bias, a_log, d_skip, norm_w, expand, tril)


def _proj_ln_kernel(a_ref, w_ref, x_ref, g_ref, b_ref, o_ref, *, alpha, parts):
    rows = a_ref.shape[0] // parts
    for r in range(parts):
        rs = slice(r * rows, (r + 1) * rows)
        h = _dot(a_ref[rs, :], w_ref[...])
        o_ref[rs, :] = _layer_norm(alpha * x_ref[rs, :] + h, g_ref[...], b_ref[...])


def _proj_ln(a, w, x2d, g, b, *, alpha, tm=1024, parts=4):
    t, k = a.shape
    d = w.shape[1]
    return pl.pallas_call(
        functools.partial(_proj_ln_kernel, alpha=alpha, parts=parts),
        grid=(t // tm,),
        in_specs=[pl.BlockSpec((tm, k), lambda i: (i, 0)),
                  _const_spec((k, d)),
                  pl.BlockSpec((tm, d), lambda i: (i, 0)),
                  _const_spec((1, d)), _const_spec((1, d))],
        out_specs=pl.BlockSpec((tm, d), lambda i: (i, 0)),
        out_shape=jax.ShapeDtypeStruct((t, d), F32),
        compiler_params=_cparams("parallel"),
        name="proj_ln",
    )(a, w, x2d, g, b)


def _ffn_kernel(x_ref, wg_ref, wu_ref, wd_ref, g_ref, b_ref, o_ref, a_scr, *, alpha, tf, parts):
    xb = x_ref[...].astype(BF16)
    d_ff = wd_ref.shape[0]
    for c in range(d_ff // tf):
        cs = slice(c * tf, (c + 1) * tf)
        a_scr[:, cs] = (_silu(_dot(xb, wg_ref[:, cs])) * _dot(xb, wu_ref[:, cs])).astype(BF16)
    rows = x_ref.shape[0] // parts
    for r in range(parts):
        rs = slice(r * rows, (r + 1) * rows)
        h = _dot(a_scr[rs, :], wd_ref[...])
        o_ref[rs, :] = _layer_norm(alpha * x_ref[rs, :] + h, g_ref[...], b_ref[...])


def _ffn(x2d, wg, wu, wd, g, b, *, alpha, tm=1024, tf=256, parts=4):
    t, d = x2d.shape
    d_ff = wd.shape[0]
    assert d_ff % tf == 0
    return pl.pallas_call(
        functools.partial(_ffn_kernel, alpha=alpha, tf=tf, parts=parts),
        grid=(t // tm,),
        in_specs=[pl.BlockSpec((tm, d), lambda i: (i, 0)),
                  _const_spec((d, d_ff)), _const_spec((d, d_ff)),
                  _const_spec((d_ff, d)),
                  _const_spec((1, d)), _const_spec((1, d))],
        out_specs=pl.BlockSpec((tm, d), lambda i: (i, 0)),
        out_shape=jax.ShapeDtypeStruct((t, d), F32),
        scratch_shapes=[pltpu.VMEM((tm, d_ff), BF16)],
        compiler_params=_cparams("parallel"),
        name="ffn_ln",
    )(x2d, wg, wu, wd, g, b)


def _qkv_kernel(x_ref, w_ref, wvt_ref, wf_ref, bf_ref, pk_ref, cq_ref, tril_ref,
                q_ref, k_ref, vt_ref, carry, *, att_dim, n_heads, scale):
    @pl.when(pl.program_id(1) == 0)
    def _():
        carry[...] = jnp.zeros(carry.shape, F32)

    xb = x_ref[...].astype(BF16)
    tm = xb.shape[0]
    lane = lax.broadcasted_iota(jnp.int32, (tm, LANES), 1)

    logf = -_softplus(-(_dot(xb, wf_ref[...]) + bf_ref[...]))
    logf = jnp.where(lane < n_heads, logf, 0.0)
    fc = _dot_f32_rhs(tril_ref[...], logf) + carry[...]
    carry[...] = fc[tm - 1:tm, :]
    hi, mid, lo = _split3(fc * LOG2E)
    feats = (hi.astype(F32) + pltpu.roll(mid.astype(F32), n_heads, axis=1)
             + pltpu.roll(lo.astype(F32), 2 * n_heads, axis=1)).astype(BF16)
    k_aug = _dot(feats, pk_ref[...])

    low = lane < ATT_HEAD_DIM
    tn = 2 * LANES
    for c in range(att_dim // tn):
        q4 = _dot(xb, w_ref[:, c * tn:(c + 1) * tn]) * scale
        k4 = _dot(xb, w_ref[:, att_dim + c * tn:att_dim + (c + 1) * tn])
        for h in range(tn // ATT_HEAD_DIM):
            ps = slice((h // 2) * LANES, (h // 2 + 1) * LANES)
            qh, kh = q4[:, ps], k4[:, ps]
            if h % 2:
                qh, kh = (pltpu.roll(a, ATT_HEAD_DIM, axis=1) for a in (qh, kh))
            head = c * (tn // ATT_HEAD_DIM) + h
            hs = slice(head * LANES, (head + 1) * LANES)
            q_ref[:, hs] = jnp.where(low, qh, cq_ref[...]).astype(BF16)
            k_ref[:, hs] = jnp.where(low, kh, k_aug[:, hs]).astype(BF16)
    vt_ref[0] = _dot_nt(wvt_ref[...], xb).astype(BF16)


def _qkv(x2d, w, wvt, wf, bf, pk, cq, tril, *, batch, seq, att_dim, n_heads, scale, tm):
    t, d = x2d.shape
    nblk = seq // tm
    wide = n_heads * LANES
    row = lambda b, s: (b * nblk + s, 0)
    kern = functools.partial(_qkv_kernel, att_dim=att_dim, n_heads=n_heads, scale=scale)
    return pl.pallas_call(
        kern,
        grid=(batch, nblk),
        in_specs=[pl.BlockSpec((tm, d), row),
                  _const_spec(w.shape), _const_spec(wvt.shape), _const_spec(wf.shape), _const_spec(bf.shape),
                  _const_spec(pk.shape), _const_spec(cq.shape), _const_spec(tril.shape)],
        out_specs=[pl.BlockSpec((tm, wide), row), pl.BlockSpec((tm, wide), row),
                   pl.BlockSpec((1, att_dim, tm), lambda b, s: (b, 0, s))],
        out_shape=[jax.ShapeDtypeStruct((t, wide), BF16), jax.ShapeDtypeStruct((t, wide), BF16),
                   jax.ShapeDtypeStruct((batch, att_dim, seq), BF16)],
        scratch_shapes=[pltpu.VMEM((1, LANES), F32)],
        compiler_params=_cparams("parallel", "arbitrary"),
        name="qkv_proj",
    )(x2d, w, wvt, wf, bf, pk, cq, tril)


def _attn_kernel(q_ref, k_ref, vt_ref, o_ref, vaug, *, tq, tk, seq):
    d = ATT_HEAD_DIM
    ones = jnp.ones((SUM_ROWS, seq), BF16)
    for hh in range(2):
        vaug[hh, 0:d, :] = vt_ref[0, hh * d:(hh + 1) * d, :]
        vaug[hh, d:d + SUM_ROWS, :] = ones

    def n_key_tiles(qi):
        return -(-(qi + 1) * tq // tk)

    def first_query(qi, j):
        return max(0, j * tk - qi * tq) // QB * QB

    def scores(qi, j, hh):
        hs = slice(hh * LANES, (hh + 1) * LANES)
        k = k_ref[0, j * tk:(j + 1) * tk, hs]
        blocks = []
        for b in range(first_query(qi, j) // QB, tq // QB):
            t0 = qi * tq + b * QB
            s = _dot_nt(k, q_ref[0, t0:t0 + QB, hs])
            if (j + 1) * tk - 1 > t0:
                row = lax.broadcasted_iota(jnp.int32, s.shape, 0)
                col = lax.broadcasted_iota(jnp.int32, s.shape, 1)
                s = jnp.where(row + j * tk <= col + t0, s, -jnp.inf)
            blocks.append(s)
        return blocks

    items = [(qi, j, hh) for qi in range(seq // tq) for j in range(n_key_tiles(qi)) for hh in range(2)]
    m_run, acc_run, outs = {}, {}, {}

    def softmax_part(s, qi, j, hh):
        b0 = first_query(qi, j) // QB
        ps, rescales = [], []
        for b in range(b0, tq // QB):
            s_b = s[b - b0]
            cm = jnp.max(s_b, axis=0, keepdims=True)
            if j == 0:
                m_new, rescale = cm, None
            else:
                m_new = jnp.maximum(m_run[hh, b], cm)
                rescale = jnp.exp2(m_run[hh, b] - m_new)
            m_run[hh, b] = m_new
            ps.append(jnp.exp2(s_b - m_new).astype(BF16))
            rescales.append(rescale)
        return ps, rescales

    def value_part(ps, rescales, qi, j, hh):
        b0 = first_query(qi, j) // QB
        va = vaug[hh, :, j * tk:(j + 1) * tk]
        for b in range(b0, tq // QB):
            pv_b = _dot(va, ps[b - b0])
            acc_run[hh, b] = pv_b if j == 0 else rescales[b - b0] * acc_run[hh, b] + pv_b
        if j == n_key_tiles(qi) - 1:
            acc = jnp.concatenate([acc_run[hh, b] for b in range(tq // QB)], axis=1)
            outs[hh] = acc[0:d] / acc[d:d + 1]
            if hh == 1:
                o_ref[0, qi * tq:(qi + 1) * tq, :] = (
                    jnp.concatenate([outs[0], outs[1]], axis=0).T.astype(o_ref.dtype))

    n = len(items)
    s_q = {0: scores(*items[0])}
    if n > 1:
        s_q[1] = scores(*items[1])
    p_q = {0: softmax_part(s_q.pop(0), *items[0])}
    for idx in range(n):
        if idx + 2 < n:
            s_q[idx + 2] = scores(*items[idx + 2])
        if idx + 1 < n:
            p_q[idx + 1] = softmax_part(s_q.pop(idx + 1), *items[idx + 1])
        value_part(*p_q.pop(idx), *items[idx])


def _attention(q, k, vt, *, batch, seq, n_heads, tq, tk):
    wide = n_heads * LANES
    q3, k3 = (a.reshape(batch, seq, wide) for a in (q, k))
    pair = lambda b, p: (b, 0, p)
    return pl.pallas_call(
        functools.partial(_attn_kernel, tq=tq, tk=tk, seq=seq),
        grid=(batch, n_heads // 2),
        in_specs=[pl.BlockSpec((1, seq, 2 * LANES), pair),
                  pl.BlockSpec((1, seq, 2 * LANES), pair),
                  pl.BlockSpec((1, 2 * ATT_HEAD_DIM, seq), lambda b, p: (b, p, 0))],
        out_specs=pl.BlockSpec((1, seq, LANES), pair),
        out_shape=jax.ShapeDtypeStruct((batch, seq, n_heads * ATT_HEAD_DIM), BF16),
        scratch_shapes=[pltpu.VMEM((2, ATT_HEAD_DIM + SUM_ROWS, seq), BF16)],
        compiler_params=_cparams("parallel", "parallel"),
        name="fox_attention",
    )(q3, k3, vt)


def _pad_lanes(a, n=LANES):
    return jnp.pad(a, [(0, 0)] * (a.ndim - 1) + [(0, n - a.shape[-1])])


def _block_tril(n, block):
    i = jnp.arange(n)
    return ((i[:, None] >= i[None, :]) & (i[:, None] // block == i[None, :] // block)).astype(BF16)


def kernel(x, ssm_in_w, ssm_conv_w, ssm_conv_b, ssm_dt_bias, ssm_a_log, ssm_d, ssm_norm_w, ssm_out_w,
           kv_w, kv_b_f, att_q_w, att_o_w, ffn_gate_w, ffn_up_w, ffn_down_w,
           ln_mix_g, ln_mix_b, ln_ffn_g, ln_ffn_b):
    batch, seq, d_model = x.shape
    depth = ffn_gate_w.shape[0]
    assert ssm_in_w.shape[0] == 1 and att_q_w.shape[0] == 1 and depth == 2
    alpha = (2.0 * depth) ** 0.25
    d_inner = ssm_out_w.shape[1]
    n_conv = ssm_conv_w.shape[2]
    att_dim = att_q_w.shape[2]
    n_att_heads = att_dim // ATT_HEAD_DIM
    tf = 256
    ssd_rows = 256
    t = batch * seq
    x2d = x.reshape(t, d_model)
    row2 = lambda a: a.reshape(1, -1).astype(F32)

    in_w = ssm_in_w[0]
    n_main = d_inner + n_conv
    zx, dt_raw = _inproj(x2d, in_w[:, :n_main].astype(BF16), _pad_lanes(in_w[:, n_main:]).astype(BF16),
                         n_gate=d_inner)
    head_of_lane = jnp.arange(d_inner) // SSM_HEAD_DIM
    expand = (jnp.arange(LANES)[:, None] == head_of_lane[None, :]).astype(BF16)
    y = _ssd(zx, dt_raw, ssm_conv_w[0], row2(ssm_conv_b[0]),
             _pad_lanes(row2(ssm_dt_bias[0])), _pad_lanes(row2(ssm_a_log[0])),
             row2(jnp.repeat(ssm_d[0], SSM_HEAD_DIM)), row2(ssm_norm_w[0]),
             expand, _block_tril(ssd_rows, CHUNK),
             batch=batch, seq=seq, d_inner=d_inner, rows=ssd_rows)
    x2d = _proj_ln(y, ssm_out_w[0].astype(BF16), x2d, row2(ln_mix_g[0]), row2(ln_mix_b[0]), alpha=alpha)
    x2d = _ffn(x2d, ffn_gate_w[0].astype(BF16), ffn_up_w[0].astype(BF16),
               ffn_down_w[0].astype(BF16), row2(ln_ffn_g[0]), row2(ln_ffn_b[0]), alpha=alpha, tf=tf)

    wqk = jnp.concatenate([att_q_w[0], kv_w[:, :att_dim]], axis=1).astype(BF16)
    wvt = kv_w[:, att_dim:2 * att_dim].T.astype(BF16)
    wf = _pad_lanes(kv_w[:, 2 * att_dim:]).astype(BF16)
    wide = n_att_heads * LANES
    feat = jnp.arange(LANES)[:, None]
    slot = jnp.arange(wide)[None, :]
    pk = sum(-((feat == part * n_att_heads + h) & (slot == h * LANES + AUG + part)).astype(F32)
             for part in range(3) for h in range(n_att_heads)).astype(BF16)
    lane = jnp.arange(LANES)[None, :]
    cq = ((lane >= AUG) & (lane < AUG + 3)).astype(F32)
    tm_qkv = 512
    q, k, vt = _qkv(x2d, wqk, wvt, wf, _pad_lanes(row2(kv_b_f)), pk, cq,
                    _block_tril(tm_qkv, tm_qkv),
                    batch=batch, seq=seq, att_dim=att_dim, n_heads=n_att_heads,
                    scale=ATT_HEAD_DIM ** -0.5 * LOG2E, tm=tm_qkv)

    o = _attention(q, k, vt, batch=batch, seq=seq, n_heads=n_att_heads, tq=1024, tk=256)
    x2d = _proj_ln(o.reshape(t, att_dim), att_o_w[0].astype(BF16), x2d,
                   row2(ln_mix_g[1]), row2(ln_mix_b[1]), alpha=alpha)
    x2d = _ffn(x2d, ffn_gate_w[1].astype(BF16), ffn_up_w[1].astype(BF16),
               ffn_down_w[1].astype(BF16), row2(ln_ffn_g[1]), row2(ln_ffn_b[1]), alpha=alpha, tf=tf)
    return x2d.reshape(batch, seq, d_model)
```
